```python
import jax, jax.numpy as jnp
from jax import lax
import numpy as np

D_MODEL = 2048
BATCH = 2
SEQ = 8192
DEPTH = 2

CONV_CH = D_MODEL // 2
CONV_WIDTH = 31
ATT_HEADS = 16
ATT_KV_HEADS = 4
ATT_HEAD_DIM = 64
ATT_WINDOW = 128
ATT_BLOCK = 128
ATT_Q_DIM = ATT_HEADS * ATT_HEAD_DIM
ATT_KV_DIM = ATT_KV_HEADS * ATT_HEAD_DIM
GLA_HEADS = 4
GLA_DK = 128
GLA_DV = 256
GLA_K_DIM = GLA_HEADS * GLA_DK
GLA_V_DIM = GLA_HEADS * GLA_DV
GLA_RANK = 16
GLA_GATE_NORM = 16.0
GLA_CHUNK = 64
D_FF = 5632
N_BRANCH = 3
EPS = 1e-6
D_IN = 2 * CONV_CH + ATT_Q_DIM + 2 * ATT_KV_DIM + 2 * GLA_K_DIM + 2 * GLA_V_DIM + GLA_RANK + N_BRANCH * D_MODEL

kernel_name = 'hybrid_conv_swa_gla_macaron'


def _split_sizes():
    return (2 * CONV_CH, ATT_Q_DIM, ATT_KV_DIM, ATT_KV_DIM, GLA_K_DIM, GLA_K_DIM,
            GLA_V_DIM, GLA_V_DIM, GLA_RANK, N_BRANCH * D_MODEL)


def rmsnorm(x, w):
    xf = x.astype(jnp.float32)
    r = lax.rsqrt(jnp.mean(xf * xf, axis=-1, keepdims=True) + EPS)
    return (xf * r).astype(x.dtype) * w


def swiglu_ffn(h, w_gate, w_up, w_down):
    return (jax.nn.silu(h @ w_gate) * (h @ w_up)) @ w_down


def conformer_conv(u, dw_w, dw_b, ln_w, ln_b, w_pw):
    a, g = jnp.split(u, 2, axis=-1)
    z = a * jax.nn.sigmoid(g)
    z = lax.conv_general_dilated(z, dw_w.astype(z.dtype), window_strides=(1,),
                                 padding=[(CONV_WIDTH - 1, 0)],
                                 dimension_numbers=('NWC', 'WIO', 'NWC'),
                                 feature_group_count=CONV_CH) + dw_b
    zf = z.astype(jnp.float32)
    mu = jnp.mean(zf, axis=-1, keepdims=True)
    var = jnp.mean(jnp.square(zf - mu), axis=-1, keepdims=True)
    z = ((zf - mu) * lax.rsqrt(var + EPS)).astype(u.dtype) * ln_w + ln_b
    return jax.nn.silu(z) @ w_pw


def alibi_slopes(n_heads):
    return jnp.exp2(-8.0 * jnp.arange(1, n_heads + 1, dtype=jnp.float32) / n_heads)


def swa_sink_attention(q, k, v, sinks):
    B, S = q.shape[0], q.shape[1]
    nb = S // ATT_BLOCK
    G = ATT_HEADS // ATT_KV_HEADS
    q = q.reshape(B, nb, ATT_BLOCK, ATT_KV_HEADS, G, ATT_HEAD_DIM)
    k = k.reshape(B, nb, ATT_BLOCK, ATT_KV_HEADS, ATT_HEAD_DIM)
    v = v.reshape(B, nb, ATT_BLOCK, ATT_KV_HEADS, ATT_HEAD_DIM)
    pad = ((0, 0), (1, 0), (0, 0), (0, 0), (0, 0))
    kb = jnp.concatenate([jnp.pad(k, pad)[:, :-1], k], axis=2)
    vb = jnp.concatenate([jnp.pad(v, pad)[:, :-1], v], axis=2)
    scores = jnp.einsum('bnqhgd,bnkhd->bnhgqk', q, kb).astype(jnp.float32) * (ATT_HEAD_DIM ** -0.5)
    kidx = jnp.arange(2 * ATT_BLOCK)
    dist = (ATT_BLOCK + jnp.arange(ATT_BLOCK))[:, None] - kidx[None, :]
    valid = (dist >= 0) & (dist < ATT_WINDOW)
    valid = valid[None] & ((jnp.arange(nb)[:, None, None] > 0) | (kidx[None, None, :] >= ATT_BLOCK))
    slopes = alibi_slopes(ATT_HEADS).reshape(ATT_KV_HEADS, G)
    bias = -slopes[:, :, None, None] * dist.astype(jnp.float32)[None, None]
    scores = jnp.where(valid[None, :, None, None], scores + bias[None, None], -jnp.inf)
    sink = sinks.astype(jnp.float32).reshape(ATT_KV_HEADS, G)[None, None, :, :, None, None]
    m = jnp.maximum(jnp.max(scores, axis=-1, keepdims=True), sink)
    p = jnp.exp(scores - m)
    probs = p / (jnp.sum(p, axis=-1, keepdims=True) + jnp.exp(sink - m))
    out = jnp.einsum('bnhgqk,bnkhd->bnqhgd', probs.astype(v.dtype), vb)
    return out.reshape(B, S, ATT_Q_DIM)


def gla_attention(q, k, v, r, a_low, w_alpha, b_alpha, norm_w):
    B, S = q.shape[0], q.shape[1]
    N = S // GLA_CHUNK
    log_alpha = jax.nn.log_sigmoid((a_low @ w_alpha + b_alpha).astype(jnp.float32)) / GLA_GATE_NORM
    qf = q.astype(jnp.float32).reshape(B, N, GLA_CHUNK, GLA_HEADS, GLA_DK) * (GLA_DK ** -0.5)
    kf = k.astype(jnp.float32).reshape(B, N, GLA_CHUNK, GLA_HEADS, GLA_DK)
    vf = v.astype(jnp.float32).reshape(B, N, GLA_CHUNK, GLA_HEADS, GLA_DV)
    b = jnp.cumsum(log_alpha.reshape(B, N, GLA_CHUNK, GLA_HEADS, GLA_DK), axis=2)
    b_last = b[:, :, -1:]
    q_t = qf * jnp.exp(b)
    k_t = kf * jnp.exp(-b)
    k_end = kf * jnp.exp(b_last - b)
    causal = jnp.tril(jnp.ones((GLA_CHUNK, GLA_CHUNK), dtype=bool))
    A = jnp.where(causal, jnp.einsum('bnihd,bnjhd->bnhij', q_t, k_t), 0.0)
    o_intra = jnp.einsum('bnhij,bnjhv->bnihv', A, vf)
    U = jnp.einsum('bnjhd,bnjhv->bnhdv', k_end, vf)
    decay = jnp.exp(b_last[:, :, 0])

    def step(state, inp):
        dec, u = inp
        return dec[..., None] * state + u, state

    init = jnp.zeros((B, GLA_HEADS, GLA_DK, GLA_DV), jnp.float32)
    _, s_prev = lax.scan(step, init, (jnp.moveaxis(decay, 1, 0), jnp.moveaxis(U, 1, 0)))
    s_prev = jnp.moveaxis(s_prev, 0, 1)
    o = o_intra + jnp.einsum('bnihd,bnhdv->bnihv', q_t, s_prev)
    o = o.reshape(B, S, GLA_HEADS, GLA_DV)
    o = o * lax.rsqrt(jnp.mean(o * o, axis=-1, keepdims=True) + EPS)
    o = o.reshape(B, S, GLA_V_DIM).astype(q.dtype) * norm_w
    return o * jax.nn.silu(r)


def setup_inputs(seed: int = 0) -> dict:
    key = jax.random.key(seed)
    ks = iter(jax.random.split(key, 40))
    L, D, F, C = DEPTH, D_MODEL, D_FF, CONV_CH

    def nrm(shape, scale):
        return jax.random.normal(next(ks), shape, jnp.float32) * scale

    def gain(shape):
        return 1.0 + nrm(shape, 0.02)

    return {
        'x': nrm((BATCH, SEQ, D), 1.0),
        'ffn1_norm': gain((L, D)),
        'ffn1_w_gate': nrm((L, D, F), D ** -0.5),
        'ffn1_w_up': nrm((L, D, F), D ** -0.5),
        'ffn1_w_down': nrm((L, F, D), F ** -0.5),
        'mix_norm': gain((L, D)),
        'w_in': nrm((L, D, D_IN), D ** -0.5),
        'b_gate': nrm((L, N_BRANCH * D), 0.01),
        'conv_dw_w': nrm((L, CONV_WIDTH, 1, C), CONV_WIDTH ** -0.5),
        'conv_dw_b': nrm((L, C), 0.01),
        'conv_ln_w': gain((L, C)),
        'conv_ln_b': nrm((L, C), 0.01),
        'conv_w_pw': nrm((L, C, D), C ** -0.5),
        'att_sinks': nrm((L, ATT_HEADS), 0.5),
        'att_w_o': nrm((L, ATT_Q_DIM, D), ATT_Q_DIM ** -0.5),
        'gla_w_alpha': nrm((L, GLA_RANK, GLA_K_DIM), GLA_RANK ** -0.5),
        'gla_b_alpha': nrm((L, GLA_K_DIM), 0.01),
        'gla_norm_w': gain((L, GLA_V_DIM)),
        'gla_w_o': nrm((L, GLA_V_DIM, D), GLA_V_DIM ** -0.5),
        'w_out': nrm((L, D, D), D ** -0.5),
        'ffn2_norm': gain((L, D)),
        'ffn2_w_gate': nrm((L, D, F), D ** -0.5),
        'ffn2_w_up': nrm((L, D, F), D ** -0.5),
        'ffn2_w_down': nrm((L, F, D), F ** -0.5),
        'final_norm': gain((D,)),
    }


def reference(x, ffn1_norm, ffn1_w_gate, ffn1_w_up, ffn1_w_down, mix_norm, w_in, b_gate,
              conv_dw_w, conv_dw_b, conv_ln_w, conv_ln_b, conv_w_pw, att_sinks, att_w_o,
              gla_w_alpha, gla_b_alpha, gla_norm_w, gla_w_o, w_out,
              ffn2_norm, ffn2_w_gate, ffn2_w_up, ffn2_w_down, final_norm):
    B, S = x.shape[0], x.shape[1]
    split_at = [int(c) for c in np.cumsum(_split_sizes())[:-1]]
    for l in range(DEPTH):
        x = x + 0.5 * swiglu_ffn(rmsnorm(x, ffn1_norm[l]), ffn1_w_gate[l], ffn1_w_up[l], ffn1_w_down[l])
        h = rmsnorm(x, mix_norm[l])
        proj = h @ w_in[l]
        (p_conv, p_q, p_k, p_v, g_q, g_k, g_v, g_r, g_a, p_gate) = jnp.split(proj, split_at, axis=-1)
        y_conv = conformer_conv(p_conv, conv_dw_w[l], conv_dw_b[l], conv_ln_w[l], conv_ln_b[l], conv_w_pw[l])
        y_att = swa_sink_attention(p_q, p_k, p_v, att_sinks[l]) @ att_w_o[l]
        y_gla = gla_attention(g_q, g_k, g_v, g_r, g_a, gla_w_alpha[l], gla_b_alpha[l], gla_norm_w[l]) @ gla_w_o[l]
        gates = jax.nn.sigmoid(p_gate + b_gate[l]).reshape(B, S, N_BRANCH, D_MODEL)
        merged = gates[:, :, 0] * y_conv + gates[:, :, 1] * y_att + gates[:, :, 2] * y_gla
        x = x + merged @ w_out[l]
        x = x + 0.5 * swiglu_ffn(rmsnorm(x, ffn2_norm[l]), ffn2_w_gate[l], ffn2_w_up[l], ffn2_w_down[l])
    return rmsnorm(x, final_norm)
```

```python
import functools

import jax
import jax.numpy as jnp
from jax import lax
from jax.experimental import pallas as pl
from jax.experimental.pallas import tpu as pltpu

BF = jnp.bfloat16
F32 = jnp.float32

D_MODEL = 2048
DEPTH = 2
CONV_CH = 1024
CONV_WIDTH = 31
ATT_HEADS = 16
ATT_KV_HEADS = 4
ATT_HEAD_DIM = 64
ATT_BLOCK = 128
ATT_Q_DIM = 1024
ATT_KV_DIM = 256
GLA_HEADS = 4
GLA_DK = 128
GLA_DV = 256
GLA_K_DIM = 512
GLA_V_DIM = 1024
GLA_RANK = 16
GLA_GATE_NORM = 16.0
GLA_CHUNK = 64
D_FF = 5632
EPS = 1e-6

VMEM_LIMIT_BYTES = 56 * 1024 * 1024
LANES = 128
SUBLANES = 8

N_PROJ = 12800
COL_GATE = 0
COL_CONV = 6144
COL_Q = 8192
COL_GV = 9216
COL_GR = 10240
COL_GQ = 11264
COL_GK = 11776
COL_K = 12288
COL_V = 12544

FFN_TM = 512
FFN_TF = 512
PROJ_TM = 512
PROJ_TN = 1280
CONV_TS = 512
CONV_HALO = 32
CONV_ROWS = 16
GLA_LB = 256
MERGE_TM = 256


def _params(*sem):
    return pltpu.CompilerParams(dimension_semantics=sem, vmem_limit_bytes=VMEM_LIMIT_BYTES)


def _rms(x):
    return x * lax.rsqrt(jnp.mean(x * x, axis=-1, keepdims=True) + EPS)


def _ffn_body(x_ref, nw_ref, wg_ref, wu_ref, wd_ref, *rest, final):
    if final:
        fw_ref, o_ref, h_ref = rest
    else:
        o_ref, h_ref = rest
    j = pl.program_id(1)

    @pl.when(j == 0)
    def _():
        h_ref[...] = (_rms(x_ref[...]) * nw_ref[...]).astype(BF)
        o_ref[...] = jnp.zeros_like(o_ref)

    h = h_ref[...]
    g = jnp.dot(h, wg_ref[...], preferred_element_type=F32)
    u = jnp.dot(h, wu_ref[...], preferred_element_type=F32)
    a = (g * jax.nn.sigmoid(g) * u).astype(BF)
    o_ref[...] += jnp.dot(a, wd_ref[...], preferred_element_type=F32)

    @pl.when(j == pl.num_programs(1) - 1)
    def _():
        y = x_ref[...] + 0.5 * o_ref[...]
        if final:
            y = _rms(y) * fw_ref[...]
        o_ref[...] = y


def _ffn(x, nw, wg, wu, wd, final_w=None):
    T, D = x.shape
    F = wg.shape[1]
    tm, tf = min(FFN_TM, T), FFN_TF
    final = final_w is not None
    in_specs = [
        pl.BlockSpec((tm, D), lambda i, j: (i, 0)),
        pl.BlockSpec((1, D), lambda i, j: (0, 0)),
        pl.BlockSpec((D, tf), lambda i, j: (0, j)),
        pl.BlockSpec((D, tf), lambda i, j: (0, j)),
        pl.BlockSpec((tf, D), lambda i, j: (j, 0)),
    ]
    args = [x, nw.reshape(1, D), wg, wu, wd]
    if final:
        in_specs.append(pl.BlockSpec((1, D), lambda i, j: (0, 0)))
        args.append(final_w.reshape(1, D))
    return pl.pallas_call(
        functools.partial(_ffn_body, final=final),
        grid=(T // tm, F // tf),
        in_specs=in_specs,
        out_specs=pl.BlockSpec((tm, D), lambda i, j: (i, 0)),
        out_shape=jax.ShapeDtypeStruct((T, D), F32),
        scratch_shapes=[pltpu.VMEM((tm, D), BF)],
        compiler_params=_params("parallel", "arbitrary"),
        name="ffn",
    )(*args)


def _inproj_body(x_ref, nw_ref, w_ref, wa_ref, wal_ref, bal_ref, p_ref, la_ref, h_ref):
    j = pl.program_id(1)

    @pl.when(j == 0)
    def _():
        h = (_rms(x_ref[...]) * nw_ref[...]).astype(BF)
        h_ref[...] = h
        a_low = jnp.dot(h, wa_ref[...], preferred_element_type=F32)
        z = jnp.dot(a_low.astype(BF), wal_ref[...], preferred_element_type=F32) + bal_ref[...]
        la_ref[...] = (jnp.minimum(z, 0.0) - jnp.log1p(jnp.exp(-jnp.abs(z)))) / GLA_GATE_NORM

    p_ref[...] = jnp.dot(h_ref[...], w_ref[...], preferred_element_type=F32).astype(BF)


def _inproj(x, nw, w, wa, wal, bal):
    T, D = x.shape
    tm, tn = min(PROJ_TM, T), PROJ_TN
    return pl.pallas_call(
        _inproj_body,
        grid=(T // tm, N_PROJ // tn),
        in_specs=[
            pl.BlockSpec((tm, D), lambda i, j: (i, 0)),
            pl.BlockSpec((1, D), lambda i, j: (0, 0)),
            pl.BlockSpec((D, tn), lambda i, j: (0, j)),
            pl.BlockSpec((D, LANES), lambda i, j: (0, 0)),
            pl.BlockSpec((LANES, GLA_K_DIM), lambda i, j: (0, 0)),
            pl.BlockSpec((1, GLA_K_DIM), lambda i, j: (0, 0)),
        ],
        out_specs=[
            pl.BlockSpec((tm, tn), lambda i, j: (i, j)),
            pl.BlockSpec((tm, GLA_K_DIM), lambda i, j: (i, 0)),
        ],
        out_shape=[
            jax.ShapeDtypeStruct((T, N_PROJ), BF),
            jax.ShapeDtypeStruct((T, GLA_K_DIM), F32),
        ],
        scratch_shapes=[pltpu.VMEM((tm, D), BF)],
        compiler_params=_params("parallel", "arbitrary"),
        name="inproj",
    )(x, nw.reshape(1, D), w, wa, wal, bal.reshape(1, GLA_K_DIM))


def _conv_body(a_ref, g_ref, w_ref, b_ref, lnw_ref, lnb_ref, o_ref, zs_ref, *, ts):
    s = pl.program_id(1)
    halo, rows = CONV_HALO, CONV_ROWS

    @pl.when(s == 0)
    def _():
        zs_ref[:, 0:halo, :] = jnp.zeros((SUBLANES, halo, CONV_CH), F32)

    @pl.when(s > 0)
    def _():
        zs_ref[:, 0:halo, :] = zs_ref[:, ts:ts + halo, :]

    a = a_ref[...].astype(F32)
    g = g_ref[...].astype(F32)
    zs_ref[0, halo:halo + ts, :] = a * jax.nn.sigmoid(g)

    def chunk(c, carry):
        r0 = pl.multiple_of(c * rows, rows)
        win = zs_ref[0, pl.ds(halo - SUBLANES + r0, rows + SUBLANES), :]
        for p in range(1, SUBLANES):
            zs_ref[p, pl.ds(halo + r0, rows), :] = win[SUBLANES - p:SUBLANES - p + rows, :]
        acc = jnp.zeros((rows, CONV_CH), F32) + b_ref[...]
        for d in range(CONV_WIDTH):
            j = CONV_WIDTH - 1 - d
            zd = zs_ref[d % SUBLANES, pl.ds(halo + r0 - SUBLANES * (d // SUBLANES), rows), :]
            acc = acc + w_ref[j:j + 1, :] * zd
        mu = jnp.mean(acc, axis=-1, keepdims=True)
        dlt = acc - mu
        var = jnp.mean(dlt * dlt, axis=-1, keepdims=True)
        zn = dlt * lax.rsqrt(var + EPS) * lnw_ref[...] + lnb_ref[...]
        o_ref[pl.ds(r0, rows), :] = (zn * jax.nn.sigmoid(zn)).astype(BF)
        return carry

    lax.fori_loop(0, ts // rows, chunk, 0)


def _conv(proj, B, S, dw_w, dw_b, ln_w, ln_b):
    T = B * S
    ts = min(CONV_TS, S)
    nS = S // ts
    ca, cg = COL_CONV // CONV_CH, COL_CONV // CONV_CH + 1
    vec = pl.BlockSpec((1, CONV_CH), lambda b, s: (0, 0))
    return pl.pallas_call(
        functools.partial(_conv_body, ts=ts),
        grid=(B, nS),
        in_specs=[
            pl.BlockSpec((ts, CONV_CH), lambda b, s: (b * nS + s, ca)),
            pl.BlockSpec((ts, CONV_CH), lambda b, s: (b * nS + s, cg)),
            pl.BlockSpec((CONV_WIDTH, CONV_CH), lambda b, s: (0, 0)),
            vec, vec, vec,
        ],
        out_specs=pl.BlockSpec((ts, CONV_CH), lambda b, s: (b * nS + s, 0)),
        out_shape=jax.ShapeDtypeStruct((T, CONV_CH), BF),
        scratch_shapes=[pltpu.VMEM((SUBLANES, ts + CONV_HALO, CONV_CH), F32)],
        compiler_params=_params("parallel", "arbitrary"),
        name="conv",
    )(proj, proj, dw_w.reshape(CONV_WIDTH, CONV_CH), dw_b.reshape(1, CONV_CH),
      ln_w.reshape(1, CONV_CH), ln_b.reshape(1, CONV_CH))


def _attn_body(sink_ref, q_ref, kp_ref, kc_ref, vp_ref, vc_ref, o_ref):
    n = pl.program_id(1)
    blk, dh = ATT_BLOCK, ATT_HEAD_DIM
    qi = lax.broadcasted_iota(jnp.int32, (blk, blk), 0)
    kj = lax.broadcasted_iota(jnp.int32, (blk, blk), 1)
    dist_c = (qi - kj).astype(F32)
    dist_p = dist_c + float(blk)
    valid_c = kj <= qi
    valid_p = (kj > qi) & (n > 0)
    scale = dh ** -0.5
    nt = (((1,), (1,)), ((), ()))
    group = ATT_HEADS // ATT_KV_HEADS
    for h in range(ATT_HEADS):
        kv = h // group
        slope = 2.0 ** (-8.0 * (h + 1) / ATT_HEADS)
        qh = q_ref[:, h * dh:(h + 1) * dh]
        kc = kc_ref[:, kv * dh:(kv + 1) * dh]
        kp = kp_ref[:, kv * dh:(kv + 1) * dh]
        vc = vc_ref[:, kv * dh:(kv + 1) * dh]
        vp = vp_ref[:, kv * dh:(kv + 1) * dh]
        sc = lax.dot_general(qh, kc, nt, preferred_element_type=F32)
        sp = lax.dot_general(qh, kp, nt, preferred_element_type=F32)
        sc = jnp.where(valid_c, sc * scale - slope * dist_c, -jnp.inf)
        sp = jnp.where(valid_p, sp * scale - slope * dist_p, -jnp.inf)
        sink = sink_ref[h]
        m = jnp.maximum(jnp.maximum(jnp.max(sc, axis=-1, keepdims=True),
                                    jnp.max(sp, axis=-1, keepdims=True)), sink)
        pc = jnp.exp(sc - m)
        pp = jnp.exp(sp - m)
        den = (jnp.sum(pc, axis=-1, keepdims=True) + jnp.sum(pp, axis=-1, keepdims=True)
               + jnp.exp(sink - m))
        o = (jnp.dot(pc.astype(BF), vc, preferred_element_type=F32)
             + jnp.dot(pp.astype(BF), vp, preferred_element_type=F32))
        o_ref[:, h * dh:(h + 1) * dh] = (o / den).astype(BF)


def _attn(proj, B, S, sinks):
    T = B * S
    nb = S // ATT_BLOCK
    cq = COL_Q // ATT_Q_DIM
    ck, cv = COL_K // ATT_KV_DIM, COL_V // ATT_KV_DIM
    cur = lambda c: (lambda b, n: (b * nb + n, c))
    prev = lambda c: (lambda b, n: (b * nb + jnp.maximum(n - 1, 0), c))
    return pl.pallas_call(
        _attn_body,
        grid=(B, nb),
        in_specs=[
            pl.BlockSpec(memory_space=pltpu.SMEM),
            pl.BlockSpec((ATT_BLOCK, ATT_Q_DIM), cur(cq)),
            pl.BlockSpec((ATT_BLOCK, ATT_KV_DIM), prev(ck)),
            pl.BlockSpec((ATT_BLOCK, ATT_KV_DIM), cur(ck)),
            pl.BlockSpec((ATT_BLOCK, ATT_KV_DIM), prev(cv)),
            pl.BlockSpec((ATT_BLOCK, ATT_KV_DIM), cur(cv)),
        ],
        out_specs=pl.BlockSpec((ATT_BLOCK, ATT_Q_DIM), lambda b, n: (b * nb + n, 0)),
        out_shape=jax.ShapeDtypeStruct((T, ATT_Q_DIM), BF),
        compiler_params=_params("parallel", "arbitrary"),
        name="attn",
    )(sinks, proj, proj, proj, proj, proj)


def _gla_body(q_ref, k_ref, v_ref, r_ref, la_ref, nw_ref, o_ref, st_ref, *, nchunk):
    ck, dk, dv = GLA_CHUNK, GLA_DK, GLA_DV

    @pl.when(pl.program_id(1) == 0)
    def _():
        st_ref[...] = jnp.zeros_like(st_ref)

    ri = lax.broadcasted_iota(jnp.int32, (ck, ck), 0)
    ci = lax.broadcasted_iota(jnp.int32, (ck, ck), 1)
    causal = ci <= ri
    tril = causal.astype(BF)
    nt = (((1,), (1,)), ((), ()))
    tn = (((0,), (0,)), ((), ()))
    qscale = dk ** -0.5

    def chunk(c, carry):
        r0 = pl.multiple_of(c * ck, ck)
        rows = pl.ds(r0, ck)
        la = la_ref[rows, :]
        hi = la.astype(BF)
        r1 = la - hi.astype(F32)
        mid = r1.astype(BF)
        lo = (r1 - mid.astype(F32)).astype(BF)
        b = (jnp.dot(tril, hi, preferred_element_type=F32)
             + jnp.dot(tril, mid, preferred_element_type=F32)
             + jnp.dot(tril, lo, preferred_element_type=F32))
        bl = b[ck - 1:ck, :]
        eb = jnp.exp(b)
        enb = jnp.exp(-b)
        ee = jnp.exp(bl - b)
        dec = jnp.exp(bl)
        for h in range(GLA_HEADS):
            sk = slice(h * dk, (h + 1) * dk)
            sv = slice(h * dv, (h + 1) * dv)
            q = q_ref[rows, sk].astype(F32) * qscale
            k = k_ref[rows, sk].astype(F32)
            v = v_ref[rows, sv]
            qt = (q * eb[:, sk]).astype(BF)
            kt = (k * enb[:, sk]).astype(BF)
            ke = (k * ee[:, sk]).astype(BF)
            a = lax.dot_general(qt, kt, nt, preferred_element_type=F32)
            a = jnp.where(causal, a, 0.0).astype(BF)
            st = st_ref[h]
            o = (jnp.dot(a, v, preferred_element_type=F32)
                 + lax.dot_general(qt, st.astype(BF), nt, preferred_element_type=F32))
            st_ref[h] = st * dec[:, sk] + lax.dot_general(v, ke, tn, preferred_element_type=F32)
            on = o * lax.rsqrt(jnp.mean(o * o, axis=-1, keepdims=True) + EPS)
            r = r_ref[rows, sv].astype(F32)
            o_ref[rows, sv] = (on * nw_ref[:, sv] * (r * jax.nn.sigmoid(r))).astype(BF)
        return carry

    lax.fori_loop(0, nchunk, chunk, 0)


def _gla(proj, la, B, S, norm_w):
    T = B * S
    lb = min(GLA_LB, S)
    nS = S // lb
    row = lambda c: (lambda b, s: (b * nS + s, c))
    return pl.pallas_call(
        functools.partial(_gla_body, nchunk=lb // GLA_CHUNK),
        grid=(B, nS),
        in_specs=[
            pl.BlockSpec((lb, GLA_K_DIM), row(COL_GQ // GLA_K_DIM)),
            pl.BlockSpec((lb, GLA_K_DIM), row(COL_GK // GLA_K_DIM)),
            pl.BlockSpec((lb, GLA_V_DIM), row(COL_GV // GLA_V_DIM)),
            pl.BlockSpec((lb, GLA_V_DIM), row(COL_GR // GLA_V_DIM)),
            pl.BlockSpec((lb, GLA_K_DIM), row(0)),
            pl.BlockSpec((1, GLA_V_DIM), lambda b, s: (0, 0)),
        ],
        out_specs=pl.BlockSpec((lb, GLA_V_DIM), row(0)),
        out_shape=jax.ShapeDtypeStruct((T, GLA_V_DIM), BF),
        scratch_shapes=[pltpu.VMEM((GLA_HEADS, GLA_DV, GLA_DK), F32)],
        compiler_params=_params("parallel", "arbitrary"),
        name="gla",
    )(proj, proj, proj, proj, la, norm_w.reshape(1, GLA_V_DIM))


def _merge_body(x_ref, ga_ref, gb_ref, gc_ref, ba_ref, bb_ref, bc_ref, uc_ref, ua_ref, ug_ref,
                wpw_ref, wao_ref, wgo_ref, wout_ref, o_ref):
    def gate(p_ref, b_ref):
        return jax.nn.sigmoid(p_ref[...].astype(F32) + b_ref[...])

    m = gate(ga_ref, ba_ref) * jnp.dot(uc_ref[...], wpw_ref[...], preferred_element_type=F32)
    m = m + gate(gb_ref, bb_ref) * jnp.dot(ua_ref[...], wao_ref[...], preferred_element_type=F32)
    m = m + gate(gc_ref, bc_ref) * jnp.dot(ug_ref[...], wgo_ref[...], preferred_element_type=F32)
    o_ref[...] = x_ref[...] + jnp.dot(m.astype(BF), wout_ref[...], preferred_element_type=F32)


def _merge(x, proj, b_gate, uc, ua, ug, w_pw, w_ao, w_go, w_out):
    T, D = x.shape
    tm = min(MERGE_TM, T)
    g0 = COL_GATE // D
    const = lambda shape: pl.BlockSpec(shape, lambda i: (0, 0), pipeline_mode=pl.Buffered(1))
    act = lambda w, c: pl.BlockSpec((tm, w), lambda i: (i, c))
    bias = lambda c: pl.BlockSpec((1, D), lambda i: (0, c))
    bg = b_gate.reshape(1, 3 * D)
    return pl.pallas_call(
        _merge_body,
        grid=(T // tm,),
        in_specs=[
            act(D, 0),
            act(D, g0), act(D, g0 + 1), act(D, g0 + 2),
            bias(0), bias(1), bias(2),
            act(CONV_CH, 0), act(ATT_Q_DIM, 0), act(GLA_V_DIM, 0),
            const((CONV_CH, D)), const((ATT_Q_DIM, D)), const((GLA_V_DIM, D)), const((D, D)),
        ],
        out_specs=act(D, 0),
        out_shape=jax.ShapeDtypeStruct((T, D), F32),
        compiler_params=_params("parallel"),
        name="merge",
    )(x, proj, proj, proj, bg, bg, bg, uc, ua, ug, w_pw, w_ao, w_go, w_out)


def _regroup_w_in(w):
    c = 2 * CONV_CH
    o_q = c
    o_k = o_q + ATT_Q_DIM
    o_v = o_k + ATT_KV_DIM
    o_gq = o_v + ATT_KV_DIM
    o_gk = o_gq + GLA_K_DIM
    o_gv = o_gk + GLA_K_DIM
    o_gr = o_gv + GLA_V_DIM
    o_ga = o_gr + GLA_V_DIM
    o_gate = o_ga + GLA_RANK
    main = jnp.concatenate([
        w[:, o_gate:], w[:, :c], w[:, o_q:o_k], w[:, o_gv:o_gr], w[:, o_gr:o_ga],
        w[:, o_gq:o_gk], w[:, o_gk:o_gv], w[:, o_k:o_v], w[:, o_v:o_gq]], axis=1).astype(BF)
    wa = jnp.pad(w[:, o_ga:o_gate], ((0, 0), (0, LANES - GLA_RANK))).astype(BF)
    return main, wa


def kernel(x, ffn1_norm, ffn1_w_gate, ffn1_w_up, ffn1_w_down, mix_norm, w_in, b_gate, conv_dw_w, conv_dw_b, conv_ln_w, conv_ln_b, conv_w_pw, att_sinks, att_w_o, gla_w_alpha, gla_b_alpha, gla_norm_w, gla_w_o, w_out, ffn2_norm, ffn2_w_gate, ffn2_w_up, ffn2_w_down, final_norm):
    B, S, D = x.shape
    xt = x.reshape(B * S, D)
    for l in range(DEPTH):
        xt = _ffn(xt, ffn1_norm[l], ffn1_w_gate[l].astype(BF), ffn1_w_up[l].astype(BF),
                  ffn1_w_down[l].astype(BF))
        w_main, w_a = _regroup_w_in(w_in[l])
        w_alpha = jnp.pad(gla_w_alpha[l], ((0, LANES - GLA_RANK), (0, 0))).astype(BF)
        proj, la = _inproj(xt, mix_norm[l], w_main, w_a, w_alpha, gla_b_alpha[l])
        uc = _conv(proj, B, S, conv_dw_w[l], conv_dw_b[l], conv_ln_w[l], conv_ln_b[l])
        ua = _attn(proj, B, S, att_sinks[l])
        ug = _gla(proj, la, B, S, gla_norm_w[l])
        xt = _merge(xt, proj, b_gate[l], uc, ua, ug, conv_w_pw[l].astype(BF), att_w_o[l].astype(BF),
                    gla_w_o[l].astype(BF), w_out[l].astype(BF))
        xt = _ffn(xt, ffn2_norm[l], ffn2_w_gate[l].astype(BF), ffn2_w_up[l].astype(BF),
                  ffn2_w_down[l].astype(BF), final_w=final_norm if l == DEPTH - 1 else None)
    return xt.reshape(B, S, D)
```

```python
import functools

import jax
import jax.numpy as jnp
from jax import lax
from jax.experimental import pallas as pl
from jax.experimental.pallas import tpu as pltpu

BF = jnp.bfloat16
F32 = jnp.float32

D_MODEL = 2048
DEPTH = 2
CONV_CH = 1024
CONV_WIDTH = 31
ATT_HEADS = 16
ATT_KV_HEADS = 4
ATT_HEAD_DIM = 64
ATT_BLOCK = 128
ATT_Q_DIM = 1024
ATT_KV_DIM = 256
GLA_HEADS = 4
GLA_DK = 128
GLA_DV = 256
GLA_K_DIM = 512
GLA_V_DIM = 1024
GLA_RANK = 16
GLA_GATE_NORM = 16.0
GLA_CHUNK = 64
D_FF = 5632
EPS = 1e-6

VMEM_LIMIT_BYTES = 56 * 1024 * 1024
LANES = 128
SUBLANES = 8

N_PROJ = 12800
COL_GATE = 0
COL_CONV = 6144
COL_Q = 8192
COL_GV = 9216
COL_GR = 10240
COL_GQ = 11264
COL_GK = 11776
COL_K = 12288
COL_V = 12544

FFN_TM = 1024
FFN_TF = 256
PROJ_TM = 1024
PROJ_TN = 1280
CONV_TS = 512
CONV_HALO = 32
CONV_ROWS = 32
GLA_LB = 256
MERGE_TM = 256


def _params(*sem):
    return pltpu.CompilerParams(dimension_semantics=sem, vmem_limit_bytes=VMEM_LIMIT_BYTES)


def _rms(x):
    return x * lax.rsqrt(jnp.mean(x * x, axis=-1, keepdims=True) + EPS)


def _ffn_body(x_ref, nw_ref, wg_ref, wu_ref, wd_ref, *rest, final):
    if final:
        fw_ref, o_ref, h_ref = rest
    else:
        o_ref, h_ref = rest
    j = pl.program_id(1)

    @pl.when(j == 0)
    def _():
        h_ref[...] = (_rms(x_ref[...]) * nw_ref[...]).astype(BF)
        o_ref[...] = jnp.zeros_like(o_ref)

    h = h_ref[...]
    g = jnp.dot(h, wg_ref[...], preferred_element_type=F32)
    u = jnp.dot(h, wu_ref[...], preferred_element_type=F32)
    a = (g * jax.nn.sigmoid(g) * u).astype(BF)
    o_ref[...] += jnp.dot(a, wd_ref[...], preferred_element_type=F32)

    @pl.when(j == pl.num_programs(1) - 1)
    def _():
        y = x_ref[...] + 0.5 * o_ref[...]
        if final:
            y = _rms(y) * fw_ref[...]
        o_ref[...] = y


def _ffn(x, nw, wg, wu, wd, final_w=None):
    T, D = x.shape
    F = wg.shape[1]
    tm, tf = min(FFN_TM, T), FFN_TF
    final = final_w is not None
    in_specs = [
        pl.BlockSpec((tm, D), lambda i, j: (i, 0)),
        pl.BlockSpec((1, D), lambda i, j: (0, 0)),
        pl.BlockSpec((D, tf), lambda i, j: (0, j)),
        pl.BlockSpec((D, tf), lambda i, j: (0, j)),
        pl.BlockSpec((tf, D), lambda i, j: (j, 0)),
    ]
    args = [x, nw.reshape(1, D), wg, wu, wd]
    if final:
        in_specs.append(pl.BlockSpec((1, D), lambda i, j: (0, 0)))
        args.append(final_w.reshape(1, D))
    return pl.pallas_call(
        functools.partial(_ffn_body, final=final),
        grid=(T // tm, F // tf),
        in_specs=in_specs,
        out_specs=pl.BlockSpec((tm, D), lambda i, j: (i, 0)),
        out_shape=jax.ShapeDtypeStruct((T, D), F32),
        scratch_shapes=[pltpu.VMEM((tm, D), BF)],
        compiler_params=_params("parallel", "arbitrary"),
        name="ffn",
    )(*args)


def _inproj_body(x_ref, nw_ref, w_ref, wa_ref, wal_ref, bal_ref, p_ref, la_ref, h_ref):
    j = pl.program_id(1)

    @pl.when(j == 0)
    def _():
        h = (_rms(x_ref[...]) * nw_ref[...]).astype(BF)
        h_ref[...] = h
        a_low = jnp.dot(h, wa_ref[...], preferred_element_type=F32)
        z = jnp.dot(a_low.astype(BF), wal_ref[...], preferred_element_type=F32) + bal_ref[...]
        la_ref[...] = (jnp.minimum(z, 0.0) - jnp.log1p(jnp.exp(-jnp.abs(z)))) / GLA_GATE_NORM

    p_ref[...] = jnp.dot(h_ref[...], w_ref[...], preferred_element_type=F32).astype(BF)


def _inproj(x, nw, w, wa, wal, bal):
    T, D = x.shape
    tm, tn = min(PROJ_TM, T), PROJ_TN
    return pl.pallas_call(
        _inproj_body,
        grid=(T // tm, N_PROJ // tn),
        in_specs=[
            pl.BlockSpec((tm, D), lambda i, j: (i, 0)),
            pl.BlockSpec((1, D), lambda i, j: (0, 0)),
            pl.BlockSpec((D, tn), lambda i, j: (0, j)),
            pl.BlockSpec((D, LANES), lambda i, j: (0, 0)),
            pl.BlockSpec((LANES, GLA_K_DIM), lambda i, j: (0, 0)),
            pl.BlockSpec((1, GLA_K_DIM), lambda i, j: (0, 0)),
        ],
        out_specs=[
            pl.BlockSpec((tm, tn), lambda i, j: (i, j)),
            pl.BlockSpec((tm, GLA_K_DIM), lambda i, j: (i, 0)),
        ],
        out_shape=[
            jax.ShapeDtypeStruct((T, N_PROJ), BF),
            jax.ShapeDtypeStruct((T, GLA_K_DIM), F32),
        ],
        scratch_shapes=[pltpu.VMEM((tm, D), BF)],
        compiler_params=_params("parallel", "arbitrary"),
        name="inproj",
    )(x, nw.reshape(1, D), w, wa, wal, bal.reshape(1, GLA_K_DIM))


def _conv_body(a_ref, g_ref, w_ref, b_ref, lnw_ref, lnb_ref, o_ref, zs_ref, wb_ref, y_ref, *, ts):
    s = pl.program_id(1)
    halo, rows = CONV_HALO, CONV_ROWS

    @pl.when(s == 0)
    def _():
        zs_ref[:, 0:halo, :] = jnp.zeros((SUBLANES, halo, CONV_CH), F32)
        for j in range(CONV_WIDTH):
            wb_ref[j] = jnp.broadcast_to(w_ref[j:j + 1, :], (SUBLANES, CONV_CH))

    @pl.when(s > 0)
    def _():
        zs_ref[:, 0:halo, :] = zs_ref[:, ts:ts + halo, :]

    a = a_ref[...].astype(F32)
    g = g_ref[...].astype(F32)
    zs_ref[0, halo:halo + ts, :] = a * jax.nn.sigmoid(g)

    def chunk(c, carry):
        r0 = pl.multiple_of(c * rows, rows)
        win = zs_ref[0, pl.ds(halo - SUBLANES + r0, rows + SUBLANES), :]
        for p in range(1, SUBLANES):
            zs_ref[p, pl.ds(halo + r0, rows), :] = win[SUBLANES - p:SUBLANES - p + rows, :]
        accs = [jnp.zeros((SUBLANES, CONV_CH), F32) + b_ref[...] for _ in range(rows // SUBLANES)]
        for d in range(CONV_WIDTH):
            wj = wb_ref[CONV_WIDTH - 1 - d]
            base = halo + r0 - SUBLANES * (d // SUBLANES)
            for i in range(rows // SUBLANES):
                zd = zs_ref[d % SUBLANES, pl.ds(base + i * SUBLANES, SUBLANES), :]
                accs[i] = accs[i] + wj * zd
        for i in range(rows // SUBLANES):
            y_ref[pl.ds(r0 + i * SUBLANES, SUBLANES), :] = accs[i]
        return carry

    lax.fori_loop(0, ts // rows, chunk, 0)

    y = y_ref[...]
    mu = jnp.mean(y, axis=-1, keepdims=True)
    dlt = y - mu
    var = jnp.mean(dlt * dlt, axis=-1, keepdims=True)
    zn = dlt * lax.rsqrt(var + EPS) * lnw_ref[...] + lnb_ref[...]
    o_ref[...] = (zn * jax.nn.sigmoid(zn)).astype(BF)


def _conv(proj, B, S, dw_w, dw_b, ln_w, ln_b):
    T = B * S
    ts = min(CONV_TS, S)
    nS = S // ts
    ca, cg = COL_CONV // CONV_CH, COL_CONV // CONV_CH + 1
    vec = pl.BlockSpec((1, CONV_CH), lambda b, s: (0, 0))
    return pl.pallas_call(
        functools.partial(_conv_body, ts=ts),
        grid=(B, nS),
        in_specs=[
            pl.BlockSpec((ts, CONV_CH), lambda b, s: (b * nS + s, ca)),
            pl.BlockSpec((ts, CONV_CH), lambda b, s: (b * nS + s, cg)),
            pl.BlockSpec((CONV_WIDTH, CONV_CH), lambda b, s: (0, 0)),
            vec, vec, vec,
        ],
        out_specs=pl.BlockSpec((ts, CONV_CH), lambda b, s: (b * nS + s, 0)),
        out_shape=jax.ShapeDtypeStruct((T, CONV_CH), BF),
        scratch_shapes=[pltpu.VMEM((SUBLANES, ts + CONV_HALO, CONV_CH), F32),
                        pltpu.VMEM((CONV_WIDTH, SUBLANES, CONV_CH), F32),
                        pltpu.VMEM((ts, CONV_CH), F32)],
        compiler_params=_params("parallel", "arbitrary"),
        name="conv",
    )(proj, proj, dw_w.reshape(CONV_WIDTH, CONV_CH), dw_b.reshape(1, CONV_CH),
      ln_w.reshape(1, CONV_CH), ln_b.reshape(1, CONV_CH))


def _attn_body(sink_ref, q_ref, kp_ref, kc_ref, vp_ref, vc_ref, o_ref):
    n = pl.program_id(1)
    blk, dh = ATT_BLOCK, ATT_HEAD_DIM
    group = ATT_HEADS // ATT_KV_HEADS
    qi = lax.broadcasted_iota(jnp.int32, (blk, blk), 0)
    kj = lax.broadcasted_iota(jnp.int32, (blk, blk), 1)
    cur = kj <= qi
    dist = ((qi - kj) & (blk - 1)).astype(F32)
    dist = jnp.where(jnp.logical_and(n == 0, kj > qi), jnp.inf, dist)
    cur_b = jnp.where(cur, 1.0, 0.0).astype(BF)
    prev_b = jnp.where(cur, 0.0, 1.0).astype(BF)
    lane = lax.broadcasted_iota(jnp.int32, (2 * blk, 2 * dh), 1)
    half_b = (jnp.where(lane < dh, 1.0, 0.0).astype(BF), jnp.where(lane < dh, 0.0, 1.0).astype(BF))
    qscale = jnp.asarray(dh ** -0.5, BF)
    nt = (((1,), (1,)), ((), ()))
    for p in range(ATT_KV_HEADS // 2):
        col = slice(p * 2 * dh, (p + 1) * 2 * dh)
        kcat = jnp.concatenate([kp_ref[:, col], kc_ref[:, col]], axis=0)
        vcat = jnp.concatenate([vp_ref[:, col], vc_ref[:, col]], axis=0)
        k_side = [kcat * hb for hb in half_b]
        v_side = [vcat * hb for hb in half_b]
        for g in range(group):
            c = p * group + g
            qcol = slice(c * 2 * dh, (c + 1) * 2 * dh)
            q2 = q_ref[:, qcol] * qscale
            acc = None
            for side in range(2):
                h = (2 * p + side) * group + g
                slope = 2.0 ** (-8.0 * (h + 1) / ATT_HEADS)
                s = lax.dot_general(q2, k_side[side], nt, preferred_element_type=F32)
                t = jnp.where(cur, s[:, blk:], s[:, :blk]) - slope * dist
                sink = sink_ref[h]
                m = jnp.maximum(jnp.max(t, axis=-1, keepdims=True), sink)
                pe = jnp.exp(t - m)
                den = jnp.sum(pe, axis=-1, keepdims=True) + jnp.exp(sink - m)
                pb = pe.astype(BF)
                pcat = jnp.concatenate([pb * prev_b, pb * cur_b], axis=1)
                o = jnp.dot(pcat, v_side[side], preferred_element_type=F32) / den
                acc = o if acc is None else acc + o
            o_ref[:, qcol] = acc.astype(BF)


def _attn(proj, B, S, sinks):
    T = B * S
    nb = S // ATT_BLOCK
    cq = COL_Q // ATT_Q_DIM
    ck, cv = COL_K // ATT_KV_DIM, COL_V // ATT_KV_DIM
    cur = lambda c: (lambda b, n: (b * nb + n, c))
    prev = lambda c: (lambda b, n: (b * nb + jnp.maximum(n - 1, 0), c))
    return pl.pallas_call(
        _attn_body,
        grid=(B, nb),
        in_specs=[
            pl.BlockSpec(memory_space=pltpu.SMEM),
            pl.BlockSpec((ATT_BLOCK, ATT_Q_DIM), cur(cq)),
            pl.BlockSpec((ATT_BLOCK, ATT_KV_DIM), prev(ck)),
            pl.BlockSpec((ATT_BLOCK, ATT_KV_DIM), cur(ck)),
            pl.BlockSpec((ATT_BLOCK, ATT_KV_DIM), prev(cv)),
            pl.BlockSpec((ATT_BLOCK, ATT_KV_DIM), cur(cv)),
        ],
        out_specs=pl.BlockSpec((ATT_BLOCK, ATT_Q_DIM), lambda b, n: (b * nb + n, 0)),
        out_shape=jax.ShapeDtypeStruct((T, ATT_Q_DIM), BF),
        compiler_params=_params("parallel", "arbitrary"),
        name="attn",
    )(sinks, proj, proj, proj, proj, proj)


def _gla_body(q_ref, k_ref, v_ref, r_ref, la_ref, nw_ref, o_ref, st_ref, *, nchunk):
    ck, dk, dv = GLA_CHUNK, GLA_DK, GLA_DV

    @pl.when(pl.program_id(1) == 0)
    def _():
        st_ref[...] = jnp.zeros_like(st_ref)

    ri = lax.broadcasted_iota(jnp.int32, (ck, ck), 0)
    ci = lax.broadcasted_iota(jnp.int32, (ck, ck), 1)
    causal = ci <= ri
    tril = causal.astype(BF)
    nt = (((1,), (1,)), ((), ()))
    tn = (((0,), (0,)), ((), ()))
    qscale = dk ** -0.5

    def chunk(c, carry):
        r0 = pl.multiple_of(c * ck, ck)
        rows = pl.ds(r0, ck)
        la = la_ref[rows, :]
        hi = la.astype(BF)
        r1 = la - hi.astype(F32)
        mid = r1.astype(BF)
        lo = (r1 - mid.astype(F32)).astype(BF)
        b = (jnp.dot(tril, hi, preferred_element_type=F32)
             + jnp.dot(tril, mid, preferred_element_type=F32)
             + jnp.dot(tril, lo, preferred_element_type=F32))
        bl = b[ck - 1:ck, :]
        eb = jnp.exp(b)
        enb = jnp.exp(-b)
        ee = jnp.exp(bl - b)
        dec = jnp.exp(bl)
        for h in range(GLA_HEADS):
            sk = slice(h * dk, (h + 1) * dk)
            sv = slice(h * dv, (h + 1) * dv)
            q = q_ref[rows, sk].astype(F32) * qscale
            k = k_ref[rows, sk].astype(F32)
            v = v_ref[rows, sv]
            qt = (q * eb[:, sk]).astype(BF)
            kt = (k * enb[:, sk]).astype(BF)
            ke = (k * ee[:, sk]).astype(BF)
            a = lax.dot_general(qt, kt, nt, preferred_element_type=F32)
            a = jnp.where(causal, a, 0.0).astype(BF)
            st = st_ref[h]
            o = (jnp.dot(a, v, preferred_element_type=F32)
                 + lax.dot_general(qt, st.astype(BF), nt, preferred_element_type=F32))
            st_ref[h] = st * dec[:, sk] + lax.dot_general(v, ke, tn, preferred_element_type=F32)
            on = o * lax.rsqrt(jnp.mean(o * o, axis=-1, keepdims=True) + EPS)
            r = r_ref[rows, sv].astype(F32)
            o_ref[rows, sv] = (on * nw_ref[:, sv] * (r * jax.nn.sigmoid(r))).astype(BF)
        return carry

    lax.fori_loop(0, nchunk, chunk, 0)


def _gla(proj, la, B, S, norm_w):
    T = B * S
    lb = min(GLA_LB, S)
    nS = S // lb
    row = lambda c: (lambda b, s: (b * nS + s, c))
    return pl.pallas_call(
        functools.partial(_gla_body, nchunk=lb // GLA_CHUNK),
        grid=(B, nS),
        in_specs=[
            pl.BlockSpec((lb, GLA_K_DIM), row(COL_GQ // GLA_K_DIM)),
            pl.BlockSpec((lb, GLA_K_DIM), row(COL_GK // GLA_K_DIM)),
            pl.BlockSpec((lb, GLA_V_DIM), row(COL_GV // GLA_V_DIM)),
            pl.BlockSpec((lb, GLA_V_DIM), row(COL_GR // GLA_V_DIM)),
            pl.BlockSpec((lb, GLA_K_DIM), row(0)),
            pl.BlockSpec((1, GLA_V_DIM), lambda b, s: (0, 0)),
        ],
        out_specs=pl.BlockSpec((lb, GLA_V_DIM), row(0)),
        out_shape=jax.ShapeDtypeStruct((T, GLA_V_DIM), BF),
        scratch_shapes=[pltpu.VMEM((GLA_HEADS, GLA_DV, GLA_DK), F32)],
        compiler_params=_params("parallel", "arbitrary"),
        name="gla",
    )(proj, proj, proj, proj, la, norm_w.reshape(1, GLA_V_DIM))


def _merge_body(x_ref, ga_ref, gb_ref, gc_ref, ba_ref, bb_ref, bc_ref, uc_ref, ua_ref, ug_ref,
                wpw_ref, wao_ref, wgo_ref, wout_ref, o_ref):
    def gate(p_ref, b_ref):
        return jax.nn.sigmoid(p_ref[...].astype(F32) + b_ref[...])

    m = gate(ga_ref, ba_ref) * jnp.dot(uc_ref[...], wpw_ref[...], preferred_element_type=F32)
    m = m + gate(gb_ref, bb_ref) * jnp.dot(ua_ref[...], wao_ref[...], preferred_element_type=F32)
    m = m + gate(gc_ref, bc_ref) * jnp.dot(ug_ref[...], wgo_ref[...], preferred_element_type=F32)
    o_ref[...] = x_ref[...] + jnp.dot(m.astype(BF), wout_ref[...], preferred_element_type=F32)


def _merge(x, proj, b_gate, uc, ua, ug, w_pw, w_ao, w_go, w_out):
    T, D = x.shape
    tm = min(MERGE_TM, T)
    g0 = COL_GATE // D
    const = lambda shape: pl.BlockSpec(shape, lambda i: (0, 0), pipeline_mode=pl.Buffered(1))
    act = lambda w, c: pl.BlockSpec((tm, w), lambda i: (i, c))
    bias = lambda c: pl.BlockSpec((1, D), lambda i: (0, c))
    bg = b_gate.reshape(1, 3 * D)
    return pl.pallas_call(
        _merge_body,
        grid=(T // tm,),
        in_specs=[
            act(D, 0),
            act(D, g0), act(D, g0 + 1), act(D, g0 + 2),
            bias(0), bias(1), bias(2),
            act(CONV_CH, 0), act(ATT_Q_DIM, 0), act(GLA_V_DIM, 0),
            const((CONV_CH, D)), const((ATT_Q_DIM, D)), const((GLA_V_DIM, D)), const((D, D)),
        ],
        out_specs=act(D, 0),
        out_shape=jax.ShapeDtypeStruct((T, D), F32),
        compiler_params=_params("parallel"),
        name="merge",
    )(x, proj, proj, proj, bg, bg, bg, uc, ua, ug, w_pw, w_ao, w_go, w_out)


def _pair_heads(w, axis):
    group = ATT_HEADS // ATT_KV_HEADS
    split = (ATT_KV_HEADS // 2, 2, group, ATT_HEAD_DIM)
    if axis == 0:
        return w.reshape(*split, w.shape[1]).transpose(0, 2, 1, 3, 4).reshape(w.shape)
    return w.reshape(w.shape[0], *split).transpose(0, 1, 3, 2, 4).reshape(w.shape)


def _regroup_w_in(w):
    c = 2 * CONV_CH
    o_q = c
    o_k = o_q + ATT_Q_DIM
    o_v = o_k + ATT_KV_DIM
    o_gq = o_v + ATT_KV_DIM
    o_gk = o_gq + GLA_K_DIM
    o_gv = o_gk + GLA_K_DIM
    o_gr = o_gv + GLA_V_DIM
    o_ga = o_gr + GLA_V_DIM
    o_gate = o_ga + GLA_RANK
    wq = _pair_heads(w[:, o_q:o_k], axis=1)
    main = jnp.concatenate([
        w[:, o_gate:], w[:, :c], wq, w[:, o_gv:o_gr], w[:, o_gr:o_ga],
        w[:, o_gq:o_gk], w[:, o_gk:o_gv], w[:, o_k:o_v], w[:, o_v:o_gq]], axis=1).astype(BF)
    wa = jnp.pad(w[:, o_ga:o_gate], ((0, 0), (0, LANES - GLA_RANK))).astype(BF)
    return main, wa


def kernel(x, ffn1_norm, ffn1_w_gate, ffn1_w_up, ffn1_w_down, mix_norm, w_in, b_gate, conv_dw_w, conv_dw_b, conv_ln_w, conv_ln_b, conv_w_pw, att_sinks, att_w_o, gla_w_alpha, gla_b_alpha, gla_norm_w, gla_w_o, w_out, ffn2_norm, ffn2_w_gate, ffn2_w_up, ffn2_w_down, final_norm):
    B, S, D = x.shape
    xt = x.reshape(B * S, D)
    for l in range(DEPTH):
        xt = _ffn(xt, ffn1_norm[l], ffn1_w_gate[l].astype(BF), ffn1_w_up[l].astype(BF),
                  ffn1_w_down[l].astype(BF))
        w_main, w_a = _regroup_w_in(w_in[l])
        w_alpha = jnp.pad(gla_w_alpha[l], ((0, LANES - GLA_RANK), (0, 0))).astype(BF)
        proj, la = _inproj(xt, mix_norm[l], w_main, w_a, w_alpha, gla_b_alpha[l])
        uc = _conv(proj, B, S, conv_dw_w[l], conv_dw_b[l], conv_ln_w[l], conv_ln_b[l])
        ua = _attn(proj, B, S, att_sinks[l])
        ug = _gla(proj, la, B, S, gla_norm_w[l])
        xt = _merge(xt, proj, b_gate[l], uc, ua, ug, conv_w_pw[l].astype(BF),
                    _pair_heads(att_w_o[l], axis=0).astype(BF),
                    gla_w_o[l].astype(BF), w_out[l].astype(BF))
        xt = _ffn(xt, ffn2_norm[l], ffn2_w_gate[l].astype(BF), ffn2_w_up[l].astype(BF),
                  ffn2_w_down[l].astype(BF), final_w=final_norm if l == DEPTH - 1 else None)
    return xt.reshape(B, S, D)
```

```python
import functools

import jax
import jax.numpy as jnp
from jax import lax
from jax.experimental import pallas as pl
from jax.experimental.pallas import tpu as pltpu

BF = jnp.bfloat16
F32 = jnp.float32

D_MODEL = 2048
DEPTH = 2
CONV_CH = 1024
CONV_WIDTH = 31
ATT_HEADS = 16
ATT_KV_HEADS = 4
ATT_HEAD_DIM = 64
ATT_BLOCK = 128
ATT_Q_DIM = 1024
ATT_KV_DIM = 256
GLA_HEADS = 4
GLA_DK = 128
GLA_DV = 256
GLA_K_DIM = 512
GLA_V_DIM = 1024
GLA_RANK = 16
GLA_GATE_NORM = 16.0
GLA_CHUNK = 64
D_FF = 5632
EPS = 1e-6

VMEM_LIMIT_BYTES = 60 * 1024 * 1024
LANES = 128
SUBLANES = 8

N_PROJ = 12800
COL_GATE = 0
COL_CONV = 6144
COL_Q = 8192
COL_GV = 9216
COL_GR = 10240
COL_GQ = 11264
COL_GK = 11776
COL_K = 12288
COL_V = 12544

FFN_TM = 1024
FFN_TF = 256
PROJ_TM = 1024
PROJ_TN = 1280
CONV_TS = 512
CONV_HALO = 32
CONV_ROWS = 32
GLA_LB = 256
MERGE_TM = 256


def _params(*sem):
    return pltpu.CompilerParams(dimension_semantics=sem, vmem_limit_bytes=VMEM_LIMIT_BYTES)


def _rms(x):
    return x * lax.rsqrt(jnp.mean(x * x, axis=-1, keepdims=True) + EPS)


def _ffn_body(x_ref, nw_ref, wg_ref, wu_ref, wd_ref, *rest, final):
    if final:
        fw_ref, o_ref, h_ref = rest
    else:
        o_ref, h_ref = rest
    j = pl.program_id(1)

    @pl.when(j == 0)
    def _():
        h_ref[...] = (_rms(x_ref[...]) * nw_ref[...]).astype(BF)
        o_ref[...] = jnp.zeros_like(o_ref)

    h = h_ref[...]
    g = jnp.dot(h, wg_ref[...].astype(BF), preferred_element_type=F32)
    u = jnp.dot(h, wu_ref[...].astype(BF), preferred_element_type=F32)
    a = (g * jax.nn.sigmoid(g) * u).astype(BF)
    o_ref[...] += jnp.dot(a, wd_ref[...].astype(BF), preferred_element_type=F32)

    @pl.when(j == pl.num_programs(1) - 1)
    def _():
        y = x_ref[...] + 0.5 * o_ref[...]
        if final:
            y = _rms(y) * fw_ref[...]
        o_ref[...] = y


def _ffn(x, nw, wg, wu, wd, layer, final_w=None):
    T, D = x.shape
    F = wg.shape[2]
    tm, tf = min(FFN_TM, T), FFN_TF
    final = final_w is not None
    in_specs = [
        pl.BlockSpec((tm, D), lambda i, j: (i, 0)),
        pl.BlockSpec((1, D), lambda i, j: (0, 0)),
        pl.BlockSpec((None, D, tf), lambda i, j: (layer, 0, j)),
        pl.BlockSpec((None, D, tf), lambda i, j: (layer, 0, j)),
        pl.BlockSpec((None, tf, D), lambda i, j: (layer, j, 0)),
    ]
    args = [x, nw.reshape(1, D), wg, wu, wd]
    if final:
        in_specs.append(pl.BlockSpec((1, D), lambda i, j: (0, 0)))
        args.append(final_w.reshape(1, D))
    return pl.pallas_call(
        functools.partial(_ffn_body, final=final),
        grid=(T // tm, F // tf),
        in_specs=in_specs,
        out_specs=pl.BlockSpec((tm, D), lambda i, j: (i, 0)),
        out_shape=jax.ShapeDtypeStruct((T, D), F32),
        scratch_shapes=[pltpu.VMEM((tm, D), BF)],
        compiler_params=_params("parallel", "arbitrary"),
        name="ffn",
    )(*args)


def _inproj_body(x_ref, nw_ref, w_ref, wa_ref, wal_ref, bal_ref, p_ref, la_ref, h_ref):
    j = pl.program_id(1)

    @pl.when(j == 0)
    def _():
        h = (_rms(x_ref[...]) * nw_ref[...]).astype(BF)
        h_ref[...] = h
        a_low = jnp.dot(h, wa_ref[...], preferred_element_type=F32)
        z = jnp.dot(a_low.astype(BF), wal_ref[...], preferred_element_type=F32) + bal_ref[...]
        la_ref[...] = (jnp.minimum(z, 0.0) - jnp.log1p(jnp.exp(-jnp.abs(z)))) / GLA_GATE_NORM

    p_ref[...] = jnp.dot(h_ref[...], w_ref[...], preferred_element_type=F32).astype(BF)


def _inproj(x, nw, w, wa, wal, bal):
    T, D = x.shape
    tm, tn = min(PROJ_TM, T), PROJ_TN
    return pl.pallas_call(
        _inproj_body,
        grid=(T // tm, N_PROJ // tn),
        in_specs=[
            pl.BlockSpec((tm, D), lambda i, j: (i, 0)),
            pl.BlockSpec((1, D), lambda i, j: (0, 0)),
            pl.BlockSpec((D, tn), lambda i, j: (0, j)),
            pl.BlockSpec((D, LANES), lambda i, j: (0, 0)),
            pl.BlockSpec((LANES, GLA_K_DIM), lambda i, j: (0, 0)),
            pl.BlockSpec((1, GLA_K_DIM), lambda i, j: (0, 0)),
        ],
        out_specs=[
            pl.BlockSpec((tm, tn), lambda i, j: (i, j)),
            pl.BlockSpec((tm, GLA_K_DIM), lambda i, j: (i, 0)),
        ],
        out_shape=[
            jax.ShapeDtypeStruct((T, N_PROJ), BF),
            jax.ShapeDtypeStruct((T, GLA_K_DIM), F32),
        ],
        scratch_shapes=[pltpu.VMEM((tm, D), BF)],
        compiler_params=_params("parallel", "arbitrary"),
        name="inproj",
    )(x, nw.reshape(1, D), w, wa, wal, bal.reshape(1, GLA_K_DIM))


def _conv_body(a_ref, g_ref, w_ref, b_ref, lnw_ref, lnb_ref, o_ref, zs_ref, wb_ref, y_ref, *, ts):
    s = pl.program_id(1)
    halo, rows = CONV_HALO, CONV_ROWS

    @pl.when(s == 0)
    def _():
        zs_ref[:, 0:halo, :] = jnp.zeros((SUBLANES, halo, CONV_CH), F32)
        for j in range(CONV_WIDTH):
            wb_ref[j] = jnp.broadcast_to(w_ref[j:j + 1, :], (SUBLANES, CONV_CH))

    @pl.when(s > 0)
    def _():
        zs_ref[:, 0:halo, :] = zs_ref[:, ts:ts + halo, :]

    a = a_ref[...].astype(F32)
    g = g_ref[...].astype(F32)
    zs_ref[0, halo:halo + ts, :] = a * jax.nn.sigmoid(g)

    def chunk(c, carry):
        r0 = pl.multiple_of(c * rows, rows)
        win = zs_ref[0, pl.ds(halo - SUBLANES + r0, rows + SUBLANES), :]
        for p in range(1, SUBLANES):
            zs_ref[p, pl.ds(halo + r0, rows), :] = win[SUBLANES - p:SUBLANES - p + rows, :]
        accs = [jnp.zeros((SUBLANES, CONV_CH), F32) + b_ref[...] for _ in range(rows // SUBLANES)]
        for d in range(CONV_WIDTH):
            wj = wb_ref[CONV_WIDTH - 1 - d]
            base = halo + r0 - SUBLANES * (d // SUBLANES)
            for i in range(rows // SUBLANES):
                zd = zs_ref[d % SUBLANES, pl.ds(base + i * SUBLANES, SUBLANES), :]
                accs[i] = accs[i] + wj * zd
        for i in range(rows // SUBLANES):
            y_ref[pl.ds(r0 + i * SUBLANES, SUBLANES), :] = accs[i]
        return carry

    lax.fori_loop(0, ts // rows, chunk, 0)

    y = y_ref[...]
    mu = jnp.mean(y, axis=-1, keepdims=True)
    dlt = y - mu
    var = jnp.mean(dlt * dlt, axis=-1, keepdims=True)
    zn = dlt * lax.rsqrt(var + EPS) * lnw_ref[...] + lnb_ref[...]
    o_ref[...] = (zn * jax.nn.sigmoid(zn)).astype(BF)


def _conv(proj, B, S, dw_w, dw_b, ln_w, ln_b):
    T = B * S
    ts = min(CONV_TS, S)
    nS = S // ts
    ca, cg = COL_CONV // CONV_CH, COL_CONV // CONV_CH + 1
    vec = pl.BlockSpec((1, CONV_CH), lambda b, s: (0, 0))
    return pl.pallas_call(
        functools.partial(_conv_body, ts=ts),
        grid=(B, nS),
        in_specs=[
            pl.BlockSpec((ts, CONV_CH), lambda b, s: (b * nS + s, ca)),
            pl.BlockSpec((ts, CONV_CH), lambda b, s: (b * nS + s, cg)),
            pl.BlockSpec((CONV_WIDTH, CONV_CH), lambda b, s: (0, 0)),
            vec, vec, vec,
        ],
        out_specs=pl.BlockSpec((ts, CONV_CH), lambda b, s: (b * nS + s, 0)),
        out_shape=jax.ShapeDtypeStruct((T, CONV_CH), BF),
        scratch_shapes=[pltpu.VMEM((SUBLANES, ts + CONV_HALO, CONV_CH), F32),
                        pltpu.VMEM((CONV_WIDTH, SUBLANES, CONV_CH), F32),
                        pltpu.VMEM((ts, CONV_CH), F32)],
        compiler_params=_params("parallel", "arbitrary"),
        name="conv",
    )(proj, proj, dw_w.reshape(CONV_WIDTH, CONV_CH), dw_b.reshape(1, CONV_CH),
      ln_w.reshape(1, CONV_CH), ln_b.reshape(1, CONV_CH))


def _attn_body(sink_ref, q_ref, kp_ref, kc_ref, vp_ref, vc_ref, o_ref):
    n = pl.program_id(1)
    blk, dh = ATT_BLOCK, ATT_HEAD_DIM
    group = ATT_HEADS // ATT_KV_HEADS
    qi = lax.broadcasted_iota(jnp.int32, (blk, blk), 0)
    kj = lax.broadcasted_iota(jnp.int32, (blk, blk), 1)
    cur = kj <= qi
    dist = ((qi - kj) & (blk - 1)).astype(F32)
    dist = jnp.where(jnp.logical_and(n == 0, kj > qi), jnp.inf, dist)
    cur_b = jnp.where(cur, 1.0, 0.0).astype(BF)
    prev_b = jnp.where(cur, 0.0, 1.0).astype(BF)
    lane = lax.broadcasted_iota(jnp.int32, (2 * blk, 2 * dh), 1)
    half_b = (jnp.where(lane < dh, 1.0, 0.0).astype(BF), jnp.where(lane < dh, 0.0, 1.0).astype(BF))
    qscale = jnp.asarray(dh ** -0.5, BF)
    nt = (((1,), (1,)), ((), ()))
    for p in range(ATT_KV_HEADS // 2):
        col = slice(p * 2 * dh, (p + 1) * 2 * dh)
        kcat = jnp.concatenate([kp_ref[:, col], kc_ref[:, col]], axis=0)
        vcat = jnp.concatenate([vp_ref[:, col], vc_ref[:, col]], axis=0)
        k_side = [kcat * hb for hb in half_b]
        v_side = [vcat * hb for hb in half_b]
        for g in range(group):
            c = p * group + g
            qcol = slice(c * 2 * dh, (c + 1) * 2 * dh)
            q2 = q_ref[:, qcol] * qscale
            acc = None
            for side in range(2):
                h = (2 * p + side) * group + g
                slope = 2.0 ** (-8.0 * (h + 1) / ATT_HEADS)
                s = lax.dot_general(q2, k_side[side], nt, preferred_element_type=F32)
                t = jnp.where(cur, s[:, blk:], s[:, :blk]) - slope * dist
                sink = sink_ref[h]
                m = jnp.maximum(jnp.max(t, axis=-1, keepdims=True), sink)
                pe = jnp.exp(t - m)
                den = jnp.sum(pe, axis=-1, keepdims=True) + jnp.exp(sink - m)
                pb = pe.astype(BF)
                pcat = jnp.concatenate([pb * prev_b, pb * cur_b], axis=1)
                o = jnp.dot(pcat, v_side[side], preferred_element_type=F32) / den
                acc = o if acc is None else acc + o
            o_ref[:, qcol] = acc.astype(BF)


def _attn(proj, B, S, sinks):
    T = B * S
    nb = S // ATT_BLOCK
    cq = COL_Q // ATT_Q_DIM
    ck, cv = COL_K // ATT_KV_DIM, COL_V // ATT_KV_DIM
    cur = lambda c: (lambda b, n: (b * nb + n, c))
    prev = lambda c: (lambda b, n: (b * nb + jnp.maximum(n - 1, 0), c))
    return pl.pallas_call(
        _attn_body,
        grid=(B, nb),
        in_specs=[
            pl.BlockSpec(memory_space=pltpu.SMEM),
            pl.BlockSpec((ATT_BLOCK, ATT_Q_DIM), cur(cq)),
            pl.BlockSpec((ATT_BLOCK, ATT_KV_DIM), prev(ck)),
            pl.BlockSpec((ATT_BLOCK, ATT_KV_DIM), cur(ck)),
            pl.BlockSpec((ATT_BLOCK, ATT_KV_DIM), prev(cv)),
            pl.BlockSpec((ATT_BLOCK, ATT_KV_DIM), cur(cv)),
        ],
        out_specs=pl.BlockSpec((ATT_BLOCK, ATT_Q_DIM), lambda b, n: (b * nb + n, 0)),
        out_shape=jax.ShapeDtypeStruct((T, ATT_Q_DIM), BF),
        compiler_params=_params("parallel", "arbitrary"),
        name="attn",
    )(sinks, proj, proj, proj, proj, proj)


def _gla_body(q_ref, k_ref, v_ref, r_ref, la_ref, nw_ref, o_ref, st_ref, *, nchunk):
    ck, dk, dv = GLA_CHUNK, GLA_DK, GLA_DV
    lb = nchunk * ck

    @pl.when(pl.program_id(1) == 0)
    def _():
        st_ref[...] = jnp.zeros_like(st_ref)

    ri = lax.broadcasted_iota(jnp.int32, (lb, lb), 0)
    ci = lax.broadcasted_iota(jnp.int32, (lb, lb), 1)
    shift = ck.bit_length() - 1
    causal = jnp.logical_and(ci <= ri, jnp.right_shift(ri, shift) == jnp.right_shift(ci, shift))
    tril = jnp.where(causal, 1.0, 0.0).astype(BF)
    nt = (((1,), (1,)), ((), ()))
    tn = (((0,), (0,)), ((), ()))
    qscale = dk ** -0.5

    la = la_ref[...]
    hi = la.astype(BF)
    r1 = la - hi.astype(F32)
    mid = r1.astype(BF)
    lo = (r1 - mid.astype(F32)).astype(BF)
    b = (jnp.dot(tril, hi, preferred_element_type=F32)
         + jnp.dot(tril, mid, preferred_element_type=F32)
         + jnp.dot(tril, lo, preferred_element_type=F32))
    chunks = [slice(c * ck, (c + 1) * ck) for c in range(nchunk)]
    b_last = [b[(c + 1) * ck - 1:(c + 1) * ck, :] for c in range(nchunk)]
    eb = jnp.exp(b)
    enb = jnp.exp(-b)
    ee = jnp.concatenate([jnp.exp(b_last[c] - b[chunks[c], :]) for c in range(nchunk)], axis=0)
    dec = [jnp.exp(b_last[c]) for c in range(nchunk)]
    for h in range(GLA_HEADS):
        sk = slice(h * dk, (h + 1) * dk)
        sv = slice(h * dv, (h + 1) * dv)
        q = q_ref[:, sk].astype(F32) * qscale
        k = k_ref[:, sk].astype(F32)
        v = v_ref[:, sv]
        qt = (q * eb[:, sk]).astype(BF)
        kt = (k * enb[:, sk]).astype(BF)
        ke = (k * ee[:, sk]).astype(BF)
        a = lax.dot_general(qt, kt, nt, preferred_element_type=F32)
        a = jnp.where(causal, a, 0.0).astype(BF)
        o = jnp.dot(a, v, preferred_element_type=F32)
        st = st_ref[h]
        inter = []
        for c in range(nchunk):
            inter.append(lax.dot_general(qt[chunks[c], :], st.astype(BF), nt, preferred_element_type=F32))
            st = st * dec[c][:, sk] + lax.dot_general(v[chunks[c], :], ke[chunks[c], :], tn,
                                                      preferred_element_type=F32)
        st_ref[h] = st
        o = o + jnp.concatenate(inter, axis=0)
        on = o * lax.rsqrt(jnp.mean(o * o, axis=-1, keepdims=True) + EPS)
        r = r_ref[:, sv].astype(F32)
        o_ref[:, sv] = (on * nw_ref[:, sv] * (r * jax.nn.sigmoid(r))).astype(BF)


def _gla(proj, la, B, S, norm_w):
    T = B * S
    lb = min(GLA_LB, S)
    nS = S // lb
    row = lambda c: (lambda b, s: (b * nS + s, c))
    return pl.pallas_call(
        functools.partial(_gla_body, nchunk=lb // GLA_CHUNK),
        grid=(B, nS),
        in_specs=[
            pl.BlockSpec((lb, GLA_K_DIM), row(COL_GQ // GLA_K_DIM)),
            pl.BlockSpec((lb, GLA_K_DIM), row(COL_GK // GLA_K_DIM)),
            pl.BlockSpec((lb, GLA_V_DIM), row(COL_GV // GLA_V_DIM)),
            pl.BlockSpec((lb, GLA_V_DIM), row(COL_GR // GLA_V_DIM)),
            pl.BlockSpec((lb, GLA_K_DIM), row(0)),
            pl.BlockSpec((1, GLA_V_DIM), lambda b, s: (0, 0)),
        ],
        out_specs=pl.BlockSpec((lb, GLA_V_DIM), row(0)),
        out_shape=jax.ShapeDtypeStruct((T, GLA_V_DIM), BF),
        scratch_shapes=[pltpu.VMEM((GLA_HEADS, GLA_DV, GLA_DK), F32)],
        compiler_params=_params("parallel", "arbitrary"),
        name="gla",
    )(proj, proj, proj, proj, la, norm_w.reshape(1, GLA_V_DIM))


def _merge_body(x_ref, ga_ref, gb_ref, gc_ref, ba_ref, bb_ref, bc_ref, uc_ref, ua_ref, ug_ref,
                wpw_ref, wao_ref, wgo_ref, wout_ref, o_ref):
    def gate(p_ref, b_ref):
        return jax.nn.sigmoid(p_ref[...].astype(F32) + b_ref[...])

    m = gate(ga_ref, ba_ref) * jnp.dot(uc_ref[...], wpw_ref[...], preferred_element_type=F32)
    m = m + gate(gb_ref, bb_ref) * jnp.dot(ua_ref[...], wao_ref[...], preferred_element_type=F32)
    m = m + gate(gc_ref, bc_ref) * jnp.dot(ug_ref[...], wgo_ref[...], preferred_element_type=F32)
    o_ref[...] = x_ref[...] + jnp.dot(m.astype(BF), wout_ref[...], preferred_element_type=F32)


def _merge(x, proj, b_gate, uc, ua, ug, w_pw, w_ao, w_go, w_out):
    T, D = x.shape
    tm = min(MERGE_TM, T)
    g0 = COL_GATE // D
    const = lambda shape: pl.BlockSpec(shape, lambda i: (0, 0), pipeline_mode=pl.Buffered(1))
    act = lambda w, c: pl.BlockSpec((tm, w), lambda i: (i, c))
    bias = lambda c: pl.BlockSpec((1, D), lambda i: (0, c))
    bg = b_gate.reshape(1, 3 * D)
    return pl.pallas_call(
        _merge_body,
        grid=(T // tm,),
        in_specs=[
            act(D, 0),
            act(D, g0), act(D, g0 + 1), act(D, g0 + 2),
            bias(0), bias(1), bias(2),
            act(CONV_CH, 0), act(ATT_Q_DIM, 0), act(GLA_V_DIM, 0),
            const((CONV_CH, D)), const((ATT_Q_DIM, D)), const((GLA_V_DIM, D)), const((D, D)),
        ],
        out_specs=act(D, 0),
        out_shape=jax.ShapeDtypeStruct((T, D), F32),
        compiler_params=_params("parallel"),
        name="merge",
    )(x, proj, proj, proj, bg, bg, bg, uc, ua, ug, w_pw, w_ao, w_go, w_out)


def _pair_heads(w, axis):
    group = ATT_HEADS // ATT_KV_HEADS
    split = (ATT_KV_HEADS // 2, 2, group, ATT_HEAD_DIM)
    if axis == 0:
        return w.reshape(*split, w.shape[1]).transpose(0, 2, 1, 3, 4).reshape(w.shape)
    return w.reshape(w.shape[0], *split).transpose(0, 1, 3, 2, 4).reshape(w.shape)


def _regroup_w_in(w):
    c = 2 * CONV_CH
    o_q = c
    o_k = o_q + ATT_Q_DIM
    o_v = o_k + ATT_KV_DIM
    o_gq = o_v + ATT_KV_DIM
    o_gk = o_gq + GLA_K_DIM
    o_gv = o_gk + GLA_K_DIM
    o_gr = o_gv + GLA_V_DIM
    o_ga = o_gr + GLA_V_DIM
    o_gate = o_ga + GLA_RANK
    wq = _pair_heads(w[:, o_q:o_k], axis=1)
    main = jnp.concatenate([
        w[:, o_gate:], w[:, :c], wq, w[:, o_gv:o_gr], w[:, o_gr:o_ga],
        w[:, o_gq:o_gk], w[:, o_gk:o_gv], w[:, o_k:o_v], w[:, o_v:o_gq]], axis=1).astype(BF)
    wa = jnp.pad(w[:, o_ga:o_gate], ((0, 0), (0, LANES - GLA_RANK))).astype(BF)
    return main, wa


def kernel(x, ffn1_norm, ffn1_w_gate, ffn1_w_up, ffn1_w_down, mix_norm, w_in, b_gate, conv_dw_w, conv_dw_b, conv_ln_w, conv_ln_b, conv_w_pw, att_sinks, att_w_o, gla_w_alpha, gla_b_alpha, gla_norm_w, gla_w_o, w_out, ffn2_norm, ffn2_w_gate, ffn2_w_up, ffn2_w_down, final_norm):
    B, S, D = x.shape
    xt = x.reshape(B * S, D)
    for l in range(DEPTH):
        xt = _ffn(xt, ffn1_norm[l], ffn1_w_gate, ffn1_w_up, ffn1_w_down, l)
        w_main, w_a = _regroup_w_in(w_in[l])
        w_alpha = jnp.pad(gla_w_alpha[l], ((0, LANES - GLA_RANK), (0, 0))).astype(BF)
        proj, la = _inproj(xt, mix_norm[l], w_main, w_a, w_alpha, gla_b_alpha[l])
        uc = _conv(proj, B, S, conv_dw_w[l], conv_dw_b[l], conv_ln_w[l], conv_ln_b[l])
        ua = _attn(proj, B, S, att_sinks[l])
        ug = _gla(proj, la, B, S, gla_norm_w[l])
        xt = _merge(xt, proj, b_gate[l], uc, ua, ug, conv_w_pw[l].astype(BF),
                    _pair_heads(att_w_o[l], axis=0).astype(BF),
                    gla_w_o[l].astype(BF), w_out[l].astype(BF))
        xt = _ffn(xt, ffn2_norm[l], ffn2_w_gate, ffn2_w_up, ffn2_w_down, l,
                  final_w=final_norm if l == DEPTH - 1 else None)
    return xt.reshape(B, S, D)
```

```python
import functools

import jax
import jax.numpy as jnp
from jax import lax
from jax.experimental import pallas as pl
from jax.experimental.pallas import tpu as pltpu

BF = jnp.bfloat16
F32 = jnp.float32

D_MODEL = 2048
DEPTH = 2
CONV_CH = 1024
CONV_WIDTH = 31
ATT_HEADS = 16
ATT_KV_HEADS = 4
ATT_HEAD_DIM = 64
ATT_BLOCK = 128
ATT_Q_DIM = 1024
ATT_KV_DIM = 256
GLA_HEADS = 4
GLA_DK = 128
GLA_DV = 256
GLA_K_DIM = 512
GLA_V_DIM = 1024
GLA_RANK = 16
GLA_GATE_NORM = 16.0
GLA_CHUNK = 64
D_FF = 5632
EPS = 1e-6

VMEM_LIMIT_BYTES = 60 * 1024 * 1024
LANES = 128
SUBLANES = 8

N_PROJ = 12800
COL_GATE = 0
COL_CONV = 6144
COL_Q = 8192
COL_GV = 9216
COL_GR = 10240
COL_GQ = 11264
COL_GK = 11776
COL_K = 12288
COL_V = 12544

FFN_TM = 1024
FFN_TF = 512
PROJ_TM = 1024
PROJ_TN = 1280
CONV_TS = 512
CONV_HALO = 32
CONV_ROWS = 32
GLA_LB = 256
MERGE_TM = 256
REGROUP_ROWS = 128


def _params(*sem):
    return pltpu.CompilerParams(dimension_semantics=sem, vmem_limit_bytes=VMEM_LIMIT_BYTES)


def _rms(x):
    return x * lax.rsqrt(jnp.mean(x * x, axis=-1, keepdims=True) + EPS)


def _ffn_body(x_hbm, nw_ref, wg_hbm, wu_hbm, wd_hbm, *rest, final, layer, tm, tf, n_i, n_j):
    if final:
        fw_ref, o_hbm, acc_ref, h_ref, wg_buf, wu_buf, wd_buf, wsem, xsem, osem = rest
    else:
        o_hbm, acc_ref, h_ref, wg_buf, wu_buf, wd_buf, wsem, xsem, osem = rest
    i = pl.program_id(0)
    j = pl.program_id(1)
    step = i * n_j + j
    buf = i % 2
    slot = step % 2

    def weight_copies(jj, sl):
        cols = pl.ds(pl.multiple_of(jj * tf, tf), tf)
        return (pltpu.make_async_copy(wg_hbm.at[layer, :, cols], wg_buf.at[sl], wsem.at[sl, 0]),
                pltpu.make_async_copy(wu_hbm.at[layer, :, cols], wu_buf.at[sl], wsem.at[sl, 1]),
                pltpu.make_async_copy(wd_hbm.at[layer, cols, :], wd_buf.at[sl], wsem.at[sl, 2]))

    def tile_rows(ii):
        return pl.ds(pl.multiple_of(ii * tm, tm), tm)

    def x_copy(ii, b):
        return pltpu.make_async_copy(x_hbm.at[tile_rows(ii), :], acc_ref.at[b], xsem.at[b])

    def out_copy(ii, b):
        return pltpu.make_async_copy(acc_ref.at[b], o_hbm.at[tile_rows(ii), :], osem.at[b])

    @pl.when(step == 0)
    def _():
        x_copy(0, 0).start()
        for cp in weight_copies(0, 0):
            cp.start()

    @pl.when(step + 1 < n_i * n_j)
    def _():
        j_next = jnp.where(j + 1 == n_j, 0, j + 1)
        for cp in weight_copies(j_next, 1 - slot):
            cp.start()

    @pl.when(j == 0)
    def _():
        x_copy(i, buf).wait()
        h_ref[...] = (_rms(acc_ref[buf]) * nw_ref[...]).astype(BF)

    @pl.when(j == 1)
    def _():
        @pl.when(i >= 1)
        def _():
            out_copy(i - 1, 1 - buf).wait()

        @pl.when(i + 1 < n_i)
        def _():
            x_copy(i + 1, 1 - buf).start()

    for cp in weight_copies(j, slot):
        cp.wait()
    h = h_ref[...]
    g = jnp.dot(h, wg_buf[slot].astype(BF), preferred_element_type=F32)
    u = jnp.dot(h, wu_buf[slot].astype(BF), preferred_element_type=F32)
    a = (0.5 * (g * jax.nn.sigmoid(g) * u)).astype(BF)
    acc_ref[buf] += jnp.dot(a, wd_buf[slot].astype(BF), preferred_element_type=F32)

    @pl.when(j == n_j - 1)
    def _():
        if final:
            acc_ref[buf] = _rms(acc_ref[buf]) * fw_ref[...]
        out_copy(i, buf).start()

    @pl.when(step == n_i * n_j - 1)
    def _():
        out_copy(i, buf).wait()


def _ffn(x, nw, wg, wu, wd, layer, final_w=None):
    T, D = x.shape
    F = wg.shape[2]
    tm, tf = min(FFN_TM, T), FFN_TF
    n_i, n_j = T // tm, F // tf
    assert n_j >= 2, "the buffer hand-over happens at d_ff step 1"
    final = final_w is not None
    hbm = pl.BlockSpec(memory_space=pl.ANY)
    vec = pl.BlockSpec((1, D), lambda i, j: (0, 0))
    in_specs = [hbm, vec, hbm, hbm, hbm]
    args = [x, nw.reshape(1, D), wg, wu, wd]
    if final:
        in_specs.append(vec)
        args.append(final_w.reshape(1, D))
    return pl.pallas_call(
        functools.partial(_ffn_body, final=final, layer=layer, tm=tm, tf=tf, n_i=n_i, n_j=n_j),
        grid=(n_i, n_j),
        in_specs=in_specs,
        out_specs=hbm,
        out_shape=jax.ShapeDtypeStruct((T, D), F32),
        scratch_shapes=[
            pltpu.VMEM((2, tm, D), F32),
            pltpu.VMEM((tm, D), BF),
            pltpu.VMEM((2, D, tf), F32),
            pltpu.VMEM((2, D, tf), F32),
            pltpu.VMEM((2, tf, D), F32),
            pltpu.SemaphoreType.DMA((2, 3)),
            pltpu.SemaphoreType.DMA((2,)),
            pltpu.SemaphoreType.DMA((2,)),
        ],
        compiler_params=_params("arbitrary", "arbitrary"),
        name="ffn",
    )(*args)


def _inproj_body(x_ref, nw_ref, w_ref, wa_ref, wal_ref, bal_ref, p_ref, la_ref, h_ref):
    j = pl.program_id(1)

    @pl.when(j == 0)
    def _():
        h = (_rms(x_ref[...]) * nw_ref[...]).astype(BF)
        h_ref[...] = h
        a_low = jnp.dot(h, wa_ref[...], preferred_element_type=F32)
        z = jnp.dot(a_low.astype(BF), wal_ref[...], preferred_element_type=F32) + bal_ref[...]
        la_ref[...] = (jnp.minimum(z, 0.0) - jnp.log1p(jnp.exp(-jnp.abs(z)))) / GLA_GATE_NORM

    p_ref[...] = jnp.dot(h_ref[...], w_ref[...], preferred_element_type=F32).astype(BF)


def _inproj(x, nw, w, wa, wal, bal):
    T, D = x.shape
    tm, tn = min(PROJ_TM, T), PROJ_TN
    return pl.pallas_call(
        _inproj_body,
        grid=(T // tm, N_PROJ // tn),
        in_specs=[
            pl.BlockSpec((tm, D), lambda i, j: (i, 0)),
            pl.BlockSpec((1, D), lambda i, j: (0, 0)),
            pl.BlockSpec((D, tn), lambda i, j: (0, j)),
            pl.BlockSpec((D, LANES), lambda i, j: (0, 0)),
            pl.BlockSpec((LANES, GLA_K_DIM), lambda i, j: (0, 0)),
            pl.BlockSpec((1, GLA_K_DIM), lambda i, j: (0, 0)),
        ],
        out_specs=[
            pl.BlockSpec((tm, tn), lambda i, j: (i, j)),
            pl.BlockSpec((tm, GLA_K_DIM), lambda i, j: (i, 0)),
        ],
        out_shape=[
            jax.ShapeDtypeStruct((T, N_PROJ), BF),
            jax.ShapeDtypeStruct((T, GLA_K_DIM), F32),
        ],
        scratch_shapes=[pltpu.VMEM((tm, D), BF)],
        compiler_params=_params("parallel", "arbitrary"),
        name="inproj",
    )(x, nw.reshape(1, D), w, wa, wal, bal.reshape(1, GLA_K_DIM))


def _conv_body(a_ref, g_ref, w_ref, b_ref, lnw_ref, lnb_ref, o_ref, zs_ref, wb_ref, y_ref, *, ts):
    s = pl.program_id(1)
    halo, rows = CONV_HALO, CONV_ROWS

    @pl.when(s == 0)
    def _():
        zs_ref[:, 0:halo, :] = jnp.zeros((SUBLANES, halo, CONV_CH), F32)
        for j in range(CONV_WIDTH):
            wb_ref[j] = jnp.broadcast_to(w_ref[j:j + 1, :], (SUBLANES, CONV_CH))

    @pl.when(s > 0)
    def _():
        zs_ref[:, 0:halo, :] = zs_ref[:, ts:ts + halo, :]

    a = a_ref[...].astype(F32)
    g = g_ref[...].astype(F32)
    zs_ref[0, halo:halo + ts, :] = a * jax.nn.sigmoid(g)

    def chunk(c, carry):
        r0 = pl.multiple_of(c * rows, rows)
        win = zs_ref[0, pl.ds(halo - SUBLANES + r0, rows + SUBLANES), :]
        for p in range(1, SUBLANES):
            zs_ref[p, pl.ds(halo + r0, rows), :] = win[SUBLANES - p:SUBLANES - p + rows, :]
        accs = [jnp.zeros((SUBLANES, CONV_CH), F32) + b_ref[...] for _ in range(rows // SUBLANES)]
        for d in range(CONV_WIDTH):
            wj = wb_ref[CONV_WIDTH - 1 - d]
            base = halo + r0 - SUBLANES * (d // SUBLANES)
            for i in range(rows // SUBLANES):
                zd = zs_ref[d % SUBLANES, pl.ds(base + i * SUBLANES, SUBLANES), :]
                accs[i] = accs[i] + wj * zd
        for i in range(rows // SUBLANES):
            y_ref[pl.ds(r0 + i * SUBLANES, SUBLANES), :] = accs[i]
        return carry

    lax.fori_loop(0, ts // rows, chunk, 0)

    y = y_ref[...]
    mu = jnp.mean(y, axis=-1, keepdims=True)
    dlt = y - mu
    var = jnp.mean(dlt * dlt, axis=-1, keepdims=True)
    zn = dlt * lax.rsqrt(var + EPS) * lnw_ref[...] + lnb_ref[...]
    o_ref[...] = (zn * jax.nn.sigmoid(zn)).astype(BF)


def _conv(proj, B, S, dw_w, dw_b, ln_w, ln_b):
    T = B * S
    ts = min(CONV_TS, S)
    nS = S // ts
    ca, cg = COL_CONV // CONV_CH, COL_CONV // CONV_CH + 1
    vec = pl.BlockSpec((1, CONV_CH), lambda b, s: (0, 0))
    return pl.pallas_call(
        functools.partial(_conv_body, ts=ts),
        grid=(B, nS),
        in_specs=[
            pl.BlockSpec((ts, CONV_CH), lambda b, s: (b * nS + s, ca)),
            pl.BlockSpec((ts, CONV_CH), lambda b, s: (b * nS + s, cg)),
            pl.BlockSpec((CONV_WIDTH, CONV_CH), lambda b, s: (0, 0)),
            vec, vec, vec,
        ],
        out_specs=pl.BlockSpec((ts, CONV_CH), lambda b, s: (b * nS + s, 0)),
        out_shape=jax.ShapeDtypeStruct((T, CONV_CH), BF),
        scratch_shapes=[pltpu.VMEM((SUBLANES, ts + CONV_HALO, CONV_CH), F32),
                        pltpu.VMEM((CONV_WIDTH, SUBLANES, CONV_CH), F32),
                        pltpu.VMEM((ts, CONV_CH), F32)],
        compiler_params=_params("parallel", "arbitrary"),
        name="conv",
    )(proj, proj, dw_w.reshape(CONV_WIDTH, CONV_CH), dw_b.reshape(1, CONV_CH),
      ln_w.reshape(1, CONV_CH), ln_b.reshape(1, CONV_CH))


def _attn_body(sink_ref, q_ref, kp_ref, kc_ref, vp_ref, vc_ref, o_ref):
    n = pl.program_id(1)
    blk, dh = ATT_BLOCK, ATT_HEAD_DIM
    group = ATT_HEADS // ATT_KV_HEADS
    qi = lax.broadcasted_iota(jnp.int32, (blk, blk), 0)
    kj = lax.broadcasted_iota(jnp.int32, (blk, blk), 1)
    cur = kj <= qi
    dist = ((qi - kj) & (blk - 1)).astype(F32)
    dist = jnp.where(jnp.logical_and(n == 0, kj > qi), jnp.inf, dist)
    cur_b = jnp.where(cur, 1.0, 0.0).astype(BF)
    prev_b = jnp.where(cur, 0.0, 1.0).astype(BF)
    lane = lax.broadcasted_iota(jnp.int32, (2 * blk, 2 * dh), 1)
    half_b = (jnp.where(lane < dh, 1.0, 0.0).astype(BF), jnp.where(lane < dh, 0.0, 1.0).astype(BF))
    qscale = jnp.asarray(dh ** -0.5, BF)
    nt = (((1,), (1,)), ((), ()))
    for p in range(ATT_KV_HEADS // 2):
        col = slice(p * 2 * dh, (p + 1) * 2 * dh)
        kcat = jnp.concatenate([kp_ref[:, col], kc_ref[:, col]], axis=0)
        vcat = jnp.concatenate([vp_ref[:, col], vc_ref[:, col]], axis=0)
        k_side = [kcat * hb for hb in half_b]
        v_side = [vcat * hb for hb in half_b]
        for g in range(group):
            c = p * group + g
            qcol = slice(c * 2 * dh, (c + 1) * 2 * dh)
            q2 = q_ref[:, qcol] * qscale
            acc = None
            for side in range(2):
                h = (2 * p + side) * group + g
                slope = 2.0 ** (-8.0 * (h + 1) / ATT_HEADS)
                s = lax.dot_general(q2, k_side[side], nt, preferred_element_type=F32)
                t = jnp.where(cur, s[:, blk:], s[:, :blk]) - slope * dist
                sink = sink_ref[h]
                m = jnp.maximum(jnp.max(t, axis=-1, keepdims=True), sink)
                pe = jnp.exp(t - m)
                den = jnp.sum(pe, axis=-1, keepdims=True) + jnp.exp(sink - m)
                pb = pe.astype(BF)
                pcat = jnp.concatenate([pb * prev_b, pb * cur_b], axis=1)
                o = jnp.dot(pcat, v_side[side], preferred_element_type=F32) / den
                acc = o if acc is None else acc + o
            o_ref[:, qcol] = acc.astype(BF)


def _attn(proj, B, S, sinks):
    T = B * S
    nb = S // ATT_BLOCK
    cq = COL_Q // ATT_Q_DIM
    ck, cv = COL_K // ATT_KV_DIM, COL_V // ATT_KV_DIM
    cur = lambda c: (lambda b, n: (b * nb + n, c))
    prev = lambda c: (lambda b, n: (b * nb + jnp.maximum(n - 1, 0), c))
    return pl.pallas_call(
        _attn_body,
        grid=(B, nb),
        in_specs=[
            pl.BlockSpec(memory_space=pltpu.SMEM),
            pl.BlockSpec((ATT_BLOCK, ATT_Q_DIM), cur(cq)),
            pl.BlockSpec((ATT_BLOCK, ATT_KV_DIM), prev(ck)),
            pl.BlockSpec((ATT_BLOCK, ATT_KV_DIM), cur(ck)),
            pl.BlockSpec((ATT_BLOCK, ATT_KV_DIM), prev(cv)),
            pl.BlockSpec((ATT_BLOCK, ATT_KV_DIM), cur(cv)),
        ],
        out_specs=pl.BlockSpec((ATT_BLOCK, ATT_Q_DIM), lambda b, n: (b * nb + n, 0)),
        out_shape=jax.ShapeDtypeStruct((T, ATT_Q_DIM), BF),
        compiler_params=_params("parallel", "arbitrary"),
        name="attn",
    )(sinks, proj, proj, proj, proj, proj)


def _gla_body(q_ref, k_ref, v_ref, r_ref, la_ref, nw_ref, o_ref, st_ref, *, nchunk):
    ck, dk, dv = GLA_CHUNK, GLA_DK, GLA_DV
    lb = nchunk * ck

    @pl.when(pl.program_id(1) == 0)
    def _():
        st_ref[...] = jnp.zeros_like(st_ref)

    ri = lax.broadcasted_iota(jnp.int32, (lb, lb), 0)
    ci = lax.broadcasted_iota(jnp.int32, (lb, lb), 1)
    shift = ck.bit_length() - 1
    causal = jnp.logical_and(ci <= ri, jnp.right_shift(ri, shift) == jnp.right_shift(ci, shift))
    tril = jnp.where(causal, 1.0, 0.0).astype(BF)
    nt = (((1,), (1,)), ((), ()))
    tn = (((0,), (0,)), ((), ()))
    qscale = dk ** -0.5

    la = la_ref[...]
    hi = la.astype(BF)
    r1 = la - hi.astype(F32)
    mid = r1.astype(BF)
    lo = (r1 - mid.astype(F32)).astype(BF)
    b = (jnp.dot(tril, hi, preferred_element_type=F32)
         + jnp.dot(tril, mid, preferred_element_type=F32)
         + jnp.dot(tril, lo, preferred_element_type=F32))
    chunks = [slice(c * ck, (c + 1) * ck) for c in range(nchunk)]
    b_last = [b[(c + 1) * ck - 1:(c + 1) * ck, :] for c in range(nchunk)]
    eb = jnp.exp(b)
    enb = jnp.exp(-b)
    ee = jnp.concatenate([jnp.exp(b_last[c] - b[chunks[c], :]) for c in range(nchunk)], axis=0)
    dec = [jnp.exp(b_last[c]) for c in range(nchunk)]
    for h in range(GLA_HEADS):
        sk = slice(h * dk, (h + 1) * dk)
        sv = slice(h * dv, (h + 1) * dv)
        q = q_ref[:, sk].astype(F32) * qscale
        k = k_ref[:, sk].astype(F32)
        v = v_ref[:, sv]
        qt = (q * eb[:, sk]).astype(BF)
        kt = (k * enb[:, sk]).astype(BF)
        ke = (k * ee[:, sk]).astype(BF)
        a = lax.dot_general(qt, kt, nt, preferred_element_type=F32)
        a = jnp.where(causal, a, 0.0).astype(BF)
        o = jnp.dot(a, v, preferred_element_type=F32)
        st = st_ref[h]
        inter = []
        for c in range(nchunk):
            inter.append(lax.dot_general(qt[chunks[c], :], st.astype(BF), nt, preferred_element_type=F32))
            st = st * dec[c][:, sk] + lax.dot_general(v[chunks[c], :], ke[chunks[c], :], tn,
                                                      preferred_element_type=F32)
        st_ref[h] = st
        o = o + jnp.concatenate(inter, axis=0)
        on = o * lax.rsqrt(jnp.mean(o * o, axis=-1, keepdims=True) + EPS)
        r = r_ref[:, sv].astype(F32)
        o_ref[:, sv] = (on * nw_ref[:, sv] * (r * jax.nn.sigmoid(r))).astype(BF)


def _gla(proj, la, B, S, norm_w):
    T = B * S
    lb = min(GLA_LB, S)
    nS = S // lb
    row = lambda c: (lambda b, s: (b * nS + s, c))
    return pl.pallas_call(
        functools.partial(_gla_body, nchunk=lb // GLA_CHUNK),
        grid=(B, nS),
        in_specs=[
            pl.BlockSpec((lb, GLA_K_DIM), row(COL_GQ // GLA_K_DIM)),
            pl.BlockSpec((lb, GLA_K_DIM), row(COL_GK // GLA_K_DIM)),
            pl.BlockSpec((lb, GLA_V_DIM), row(COL_GV // GLA_V_DIM)),
            pl.BlockSpec((lb, GLA_V_DIM), row(COL_GR // GLA_V_DIM)),
            pl.BlockSpec((lb, GLA_K_DIM), row(0)),
            pl.BlockSpec((1, GLA_V_DIM), lambda b, s: (0, 0)),
        ],
        out_specs=pl.BlockSpec((lb, GLA_V_DIM), row(0)),
        out_shape=jax.ShapeDtypeStruct((T, GLA_V_DIM), BF),
        scratch_shapes=[pltpu.VMEM((GLA_HEADS, GLA_DV, GLA_DK), F32)],
        compiler_params=_params("parallel", "arbitrary"),
        name="gla",
    )(proj, proj, proj, proj, la, norm_w.reshape(1, GLA_V_DIM))


def _merge_body(x_ref, ga_ref, gb_ref, gc_ref, ba_ref, bb_ref, bc_ref, uc_ref, ua_ref, ug_ref,
                wpw_ref, wao_ref, wgo_ref, wout_ref, o_ref):
    def gate(p_ref, b_ref):
        return jax.nn.sigmoid(p_ref[...].astype(F32) + b_ref[...])

    m = gate(ga_ref, ba_ref) * jnp.dot(uc_ref[...], wpw_ref[...], preferred_element_type=F32)
    m = m + gate(gb_ref, bb_ref) * jnp.dot(ua_ref[...], wao_ref[...], preferred_element_type=F32)
    m = m + gate(gc_ref, bc_ref) * jnp.dot(ug_ref[...], wgo_ref[...], preferred_element_type=F32)
    o_ref[...] = x_ref[...] + jnp.dot(m.astype(BF), wout_ref[...], preferred_element_type=F32)


def _merge(x, proj, b_gate, uc, ua, ug, w_pw, w_ao, w_go, w_out):
    T, D = x.shape
    tm = min(MERGE_TM, T)
    g0 = COL_GATE // D
    const = lambda shape: pl.BlockSpec(shape, lambda i: (0, 0), pipeline_mode=pl.Buffered(1))
    act = lambda w, c: pl.BlockSpec((tm, w), lambda i: (i, c))
    bias = lambda c: pl.BlockSpec((1, D), lambda i: (0, c))
    bg = b_gate.reshape(1, 3 * D)
    return pl.pallas_call(
        _merge_body,
        grid=(T // tm,),
        in_specs=[
            act(D, 0),
            act(D, g0), act(D, g0 + 1), act(D, g0 + 2),
            bias(0), bias(1), bias(2),
            act(CONV_CH, 0), act(ATT_Q_DIM, 0), act(GLA_V_DIM, 0),
            const((CONV_CH, D)), const((ATT_Q_DIM, D)), const((GLA_V_DIM, D)), const((D, D)),
        ],
        out_specs=act(D, 0),
        out_shape=jax.ShapeDtypeStruct((T, D), F32),
        compiler_params=_params("parallel"),
        name="merge",
    )(x, proj, proj, proj, bg, bg, bg, uc, ua, ug, w_pw, w_ao, w_go, w_out)


def _pair_heads(w, axis):
    group = ATT_HEADS // ATT_KV_HEADS
    split = (ATT_KV_HEADS // 2, 2, group, ATT_HEAD_DIM)
    if axis == 0:
        return w.reshape(*split, w.shape[1]).transpose(0, 2, 1, 3, 4).reshape(w.shape)
    return w.reshape(w.shape[0], *split).transpose(0, 1, 3, 2, 4).reshape(w.shape)


def _regroup_body(w_ref, main_ref, wa_ref):
    c = 2 * CONV_CH
    o_q = c
    o_k = o_q + ATT_Q_DIM
    o_v = o_k + ATT_KV_DIM
    o_gq = o_v + ATT_KV_DIM
    o_gk = o_gq + GLA_K_DIM
    o_gv = o_gk + GLA_K_DIM
    o_gr = o_gv + GLA_V_DIM
    o_ga = o_gr + GLA_V_DIM
    o_gate = o_ga + GLA_RANK

    def put(dst, src, n):
        main_ref[:, dst:dst + n] = w_ref[:, src:src + n].astype(BF)

    put(COL_GATE, o_gate, 3 * D_MODEL)
    put(COL_CONV, 0, c)
    group, dh = ATT_HEADS // ATT_KV_HEADS, ATT_HEAD_DIM
    for col in range(ATT_HEADS // 2):
        p, g = divmod(col, group)
        for side in range(2):
            h = (2 * p + side) * group + g
            put(COL_Q + (2 * col + side) * dh, o_q + h * dh, dh)
    put(COL_GV, o_gv, GLA_V_DIM)
    put(COL_GR, o_gr, GLA_V_DIM)
    put(COL_GQ, o_gq, GLA_K_DIM)
    put(COL_GK, o_gk, GLA_K_DIM)
    put(COL_K, o_k, ATT_KV_DIM)
    put(COL_V, o_v, ATT_KV_DIM)
    rows = w_ref.shape[0]
    wa_ref[...] = jnp.concatenate(
        [w_ref[:, o_ga:o_gate], jnp.zeros((rows, LANES - GLA_RANK), F32)], axis=1).astype(BF)


def _regroup_w_in(w_in, layer):
    _, D, d_in = w_in.shape
    tr = REGROUP_ROWS
    return pl.pallas_call(
        _regroup_body,
        grid=(D // tr,),
        in_specs=[pl.BlockSpec((None, tr, d_in), lambda i: (layer, i, 0))],
        out_specs=[pl.BlockSpec((tr, N_PROJ), lambda i: (i, 0)),
                   pl.BlockSpec((tr, LANES), lambda i: (i, 0))],
        out_shape=[jax.ShapeDtypeStruct((D, N_PROJ), BF), jax.ShapeDtypeStruct((D, LANES), BF)],
        compiler_params=_params("parallel"),
        name="regroup",
    )(w_in)


def kernel(x, ffn1_norm, ffn1_w_gate, ffn1_w_up, ffn1_w_down, mix_norm, w_in, b_gate, conv_dw_w, conv_dw_b, conv_ln_w, conv_ln_b, conv_w_pw, att_sinks, att_w_o, gla_w_alpha, gla_b_alpha, gla_norm_w, gla_w_o, w_out, ffn2_norm, ffn2_w_gate, ffn2_w_up, ffn2_w_down, final_norm):
    B, S, D = x.shape
    xt = x.reshape(B * S, D)
    for l in range(DEPTH):
        xt = _ffn(xt, ffn1_norm[l], ffn1_w_gate, ffn1_w_up, ffn1_w_down, l)
        w_main, w_a = _regroup_w_in(w_in, l)
        w_alpha = jnp.pad(gla_w_alpha[l], ((0, LANES - GLA_RANK), (0, 0))).astype(BF)
        proj, la = _inproj(xt, mix_norm[l], w_main, w_a, w_alpha, gla_b_alpha[l])
        uc = _conv(proj, B, S, conv_dw_w[l], conv_dw_b[l], conv_ln_w[l], conv_ln_b[l])
        ua = _attn(proj, B, S, att_sinks[l])
        ug = _gla(proj, la, B, S, gla_norm_w[l])
        xt = _merge(xt, proj, b_gate[l], uc, ua, ug, conv_w_pw[l].astype(BF),
                    _pair_heads(att_w_o[l], axis=0).astype(BF),
                    gla_w_o[l].astype(BF), w_out[l].astype(BF))
        xt = _ffn(xt, ffn2_norm[l], ffn2_w_gate, ffn2_w_up, ffn2_w_down, l,
                  final_w=final_norm if l == DEPTH - 1 else None)
    return xt.reshape(B, S, D)
```

```python
import functools

import jax
import jax.numpy as jnp
from jax import lax
from jax.experimental import pallas as pl
from jax.experimental.pallas import tpu as pltpu

BF = jnp.bfloat16
F32 = jnp.float32

D_MODEL = 2048
DEPTH = 2
CONV_CH = 1024
CONV_WIDTH = 31
ATT_HEADS = 16
ATT_KV_HEADS = 4
ATT_HEAD_DIM = 64
ATT_BLOCK = 128
ATT_Q_DIM = 1024
ATT_KV_DIM = 256
GLA_HEADS = 4
GLA_DK = 128
GLA_DV = 256
GLA_K_DIM = 512
GLA_V_DIM = 1024
GLA_RANK = 16
GLA_GATE_NORM = 16.0
GLA_CHUNK = 64
D_FF = 5632
EPS = 1e-6

VMEM_LIMIT_BYTES = 60 * 1024 * 1024
LANES = 128
SUBLANES = 8

N_PROJ = 12800
COL_GATE = 0
COL_CONV = 6144
COL_Q = 8192
COL_GV = 9216
COL_GR = 10240
COL_GQ = 11264
COL_GK = 11776
COL_K = 12288
COL_V = 12544

FFN_TM = 1024
FFN_TF = 512
PROJ_TM = 1024
PROJ_TN = 1280
CONV_TS = 512
CONV_HALO = 32
CONV_ROWS = 32
GLA_LB = 256
MERGE_TM = 256
REGROUP_COLS = 256


def _params(*sem):
    return pltpu.CompilerParams(dimension_semantics=sem, vmem_limit_bytes=VMEM_LIMIT_BYTES)


def _rms(x):
    return x * lax.rsqrt(jnp.mean(x * x, axis=-1, keepdims=True) + EPS)


def _ffn_body(x_hbm, nw_ref, wg_hbm, wu_hbm, wd_hbm, *rest, final, layer, tm, tf, n_i, n_j):
    if final:
        fw_ref, o_hbm, acc_ref, h_ref, wg_buf, wu_buf, wd_buf, wsem, xsem, osem = rest
    else:
        o_hbm, acc_ref, h_ref, wg_buf, wu_buf, wd_buf, wsem, xsem, osem = rest
    i = pl.program_id(0)
    j = pl.program_id(1)
    step = i * n_j + j
    buf = i % 2
    slot = step % 2

    def weight_copies(jj, sl):
        cols = pl.ds(pl.multiple_of(jj * tf, tf), tf)
        return (pltpu.make_async_copy(wg_hbm.at[layer, :, cols], wg_buf.at[sl], wsem.at[sl, 0]),
                pltpu.make_async_copy(wu_hbm.at[layer, :, cols], wu_buf.at[sl], wsem.at[sl, 1]),
                pltpu.make_async_copy(wd_hbm.at[layer, cols, :], wd_buf.at[sl], wsem.at[sl, 2]))

    def tile_rows(ii):
        return pl.ds(pl.multiple_of(ii * tm, tm), tm)

    def x_copy(ii, b):
        return pltpu.make_async_copy(x_hbm.at[tile_rows(ii), :], acc_ref.at[b], xsem.at[b])

    def out_copy(ii, b):
        return pltpu.make_async_copy(acc_ref.at[b], o_hbm.at[tile_rows(ii), :], osem.at[b])

    @pl.when(step == 0)
    def _():
        x_copy(0, 0).start()
        for cp in weight_copies(0, 0):
            cp.start()

    @pl.when(step + 1 < n_i * n_j)
    def _():
        j_next = jnp.where(j + 1 == n_j, 0, j + 1)
        for cp in weight_copies(j_next, 1 - slot):
            cp.start()

    @pl.when(j == 0)
    def _():
        x_copy(i, buf).wait()
        h_ref[...] = (_rms(acc_ref[buf]) * nw_ref[...]).astype(BF)

    @pl.when(j == 1)
    def _():
        @pl.when(i >= 1)
        def _():
            out_copy(i - 1, 1 - buf).wait()

        @pl.when(i + 1 < n_i)
        def _():
            x_copy(i + 1, 1 - buf).start()

    for cp in weight_copies(j, slot):
        cp.wait()
    h = h_ref[...]
    g = jnp.dot(h, wg_buf[slot].astype(BF), preferred_element_type=F32)
    u = jnp.dot(h, wu_buf[slot].astype(BF), preferred_element_type=F32)
    a = (0.5 * (g * jax.nn.sigmoid(g) * u)).astype(BF)
    acc_ref[buf] += jnp.dot(a, wd_buf[slot].astype(BF), preferred_element_type=F32)

    @pl.when(j == n_j - 1)
    def _():
        if final:
            acc_ref[buf] = _rms(acc_ref[buf]) * fw_ref[...]
        out_copy(i, buf).start()

    @pl.when(step == n_i * n_j - 1)
    def _():
        out_copy(i, buf).wait()


def _ffn(x, nw, wg, wu, wd, layer, final_w=None):
    T, D = x.shape
    F = wg.shape[2]
    tm, tf = min(FFN_TM, T), FFN_TF
    n_i, n_j = T // tm, F // tf
    assert n_j >= 2, "the buffer hand-over happens at d_ff step 1"
    final = final_w is not None
    hbm = pl.BlockSpec(memory_space=pl.ANY)
    vec = pl.BlockSpec((1, D), lambda i, j: (0, 0))
    in_specs = [hbm, vec, hbm, hbm, hbm]
    args = [x, nw.reshape(1, D), wg, wu, wd]
    if final:
        in_specs.append(vec)
        args.append(final_w.reshape(1, D))
    return pl.pallas_call(
        functools.partial(_ffn_body, final=final, layer=layer, tm=tm, tf=tf, n_i=n_i, n_j=n_j),
        grid=(n_i, n_j),
        in_specs=in_specs,
        out_specs=hbm,
        out_shape=jax.ShapeDtypeStruct((T, D), F32),
        scratch_shapes=[
            pltpu.VMEM((2, tm, D), F32),
            pltpu.VMEM((tm, D), BF),
            pltpu.VMEM((2, D, tf), F32),
            pltpu.VMEM((2, D, tf), F32),
            pltpu.VMEM((2, tf, D), F32),
            pltpu.SemaphoreType.DMA((2, 3)),
            pltpu.SemaphoreType.DMA((2,)),
            pltpu.SemaphoreType.DMA((2,)),
        ],
        compiler_params=_params("arbitrary", "arbitrary"),
        name="ffn",
    )(*args)


def _inproj_body(x_ref, nw_ref, w_ref, wa_ref, wal_ref, bal_ref, p_ref, la_ref, h_ref):
    j = pl.program_id(1)
    nt = (((1,), (1,)), ((), ()))

    @pl.when(j == 0)
    def _():
        h = (_rms(x_ref[...]) * nw_ref[...]).astype(BF)
        h_ref[...] = h
        a_low = lax.dot_general(h, wa_ref[...], nt, preferred_element_type=F32)
        z = jnp.dot(a_low.astype(BF), wal_ref[...], preferred_element_type=F32) + bal_ref[...]
        la_ref[...] = (jnp.minimum(z, 0.0) - jnp.log1p(jnp.exp(-jnp.abs(z)))) / GLA_GATE_NORM

    p_ref[...] = lax.dot_general(h_ref[...], w_ref[...], nt, preferred_element_type=F32).astype(BF)


def _inproj(x, nw, w, wa, wal, bal):
    T, D = x.shape
    tm, tn = min(PROJ_TM, T), PROJ_TN
    return pl.pallas_call(
        _inproj_body,
        grid=(T // tm, N_PROJ // tn),
        in_specs=[
            pl.BlockSpec((tm, D), lambda i, j: (i, 0)),
            pl.BlockSpec((1, D), lambda i, j: (0, 0)),
            pl.BlockSpec((tn, D), lambda i, j: (j, 0)),
            pl.BlockSpec((LANES, D), lambda i, j: (0, 0)),
            pl.BlockSpec((LANES, GLA_K_DIM), lambda i, j: (0, 0)),
            pl.BlockSpec((1, GLA_K_DIM), lambda i, j: (0, 0)),
        ],
        out_specs=[
            pl.BlockSpec((tm, tn), lambda i, j: (i, j)),
            pl.BlockSpec((tm, GLA_K_DIM), lambda i, j: (i, 0)),
        ],
        out_shape=[
            jax.ShapeDtypeStruct((T, N_PROJ), BF),
            jax.ShapeDtypeStruct((T, GLA_K_DIM), F32),
        ],
        scratch_shapes=[pltpu.VMEM((tm, D), BF)],
        compiler_params=_params("parallel", "arbitrary"),
        name="inproj",
    )(x, nw.reshape(1, D), w, wa, wal, bal.reshape(1, GLA_K_DIM))


def _conv_body(a_ref, g_ref, w_ref, b_ref, lnw_ref, lnb_ref, o_ref, zs_ref, wb_ref, y_ref, *, ts):
    s = pl.program_id(1)
    halo, rows = CONV_HALO, CONV_ROWS

    @pl.when(s == 0)
    def _():
        zs_ref[:, 0:halo, :] = jnp.zeros((SUBLANES, halo, CONV_CH), F32)
        for j in range(CONV_WIDTH):
            wb_ref[j] = jnp.broadcast_to(w_ref[j:j + 1, :], (SUBLANES, CONV_CH))

    @pl.when(s > 0)
    def _():
        zs_ref[:, 0:halo, :] = zs_ref[:, ts:ts + halo, :]

    a = a_ref[...].astype(F32)
    g = g_ref[...].astype(F32)
    zs_ref[0, halo:halo + ts, :] = a * jax.nn.sigmoid(g)

    def chunk(c, carry):
        r0 = pl.multiple_of(c * rows, rows)
        win = zs_ref[0, pl.ds(halo - SUBLANES + r0, rows + SUBLANES), :]
        for p in range(1, SUBLANES):
            zs_ref[p, pl.ds(halo + r0, rows), :] = win[SUBLANES - p:SUBLANES - p + rows, :]
        accs = [jnp.zeros((SUBLANES, CONV_CH), F32) + b_ref[...] for _ in range(rows // SUBLANES)]
        for d in range(CONV_WIDTH):
            wj = wb_ref[CONV_WIDTH - 1 - d]
            base = halo + r0 - SUBLANES * (d // SUBLANES)
            for i in range(rows // SUBLANES):
                zd = zs_ref[d % SUBLANES, pl.ds(base + i * SUBLANES, SUBLANES), :]
                accs[i] = accs[i] + wj * zd
        for i in range(rows // SUBLANES):
            y_ref[pl.ds(r0 + i * SUBLANES, SUBLANES), :] = accs[i]
        return carry

    lax.fori_loop(0, ts // rows, chunk, 0)

    y = y_ref[...]
    mu = jnp.mean(y, axis=-1, keepdims=True)
    dlt = y - mu
    var = jnp.mean(dlt * dlt, axis=-1, keepdims=True)
    zn = dlt * lax.rsqrt(var + EPS) * lnw_ref[...] + lnb_ref[...]
    o_ref[...] = (zn * jax.nn.sigmoid(zn)).astype(BF)


def _conv(proj, B, S, dw_w, dw_b, ln_w, ln_b):
    T = B * S
    ts = min(CONV_TS, S)
    nS = S // ts
    ca, cg = COL_CONV // CONV_CH, COL_CONV // CONV_CH + 1
    vec = pl.BlockSpec((1, CONV_CH), lambda b, s: (0, 0))
    return pl.pallas_call(
        functools.partial(_conv_body, ts=ts),
        grid=(B, nS),
        in_specs=[
            pl.BlockSpec((ts, CONV_CH), lambda b, s: (b * nS + s, ca)),
            pl.BlockSpec((ts, CONV_CH), lambda b, s: (b * nS + s, cg)),
            pl.BlockSpec((CONV_WIDTH, CONV_CH), lambda b, s: (0, 0)),
            vec, vec, vec,
        ],
        out_specs=pl.BlockSpec((ts, CONV_CH), lambda b, s: (b * nS + s, 0)),
        out_shape=jax.ShapeDtypeStruct((T, CONV_CH), BF),
        scratch_shapes=[pltpu.VMEM((SUBLANES, ts + CONV_HALO, CONV_CH), F32),
                        pltpu.VMEM((CONV_WIDTH, SUBLANES, CONV_CH), F32),
                        pltpu.VMEM((ts, CONV_CH), F32)],
        compiler_params=_params("parallel", "arbitrary"),
        name="conv",
    )(proj, proj, dw_w.reshape(CONV_WIDTH, CONV_CH), dw_b.reshape(1, CONV_CH),
      ln_w.reshape(1, CONV_CH), ln_b.reshape(1, CONV_CH))


def _attn_body(sink_ref, q_ref, kp_ref, kc_ref, vp_ref, vc_ref, o_ref):
    n = pl.program_id(1)
    blk, dh = ATT_BLOCK, ATT_HEAD_DIM
    group = ATT_HEADS // ATT_KV_HEADS
    qi = lax.broadcasted_iota(jnp.int32, (blk, blk), 0)
    kj = lax.broadcasted_iota(jnp.int32, (blk, blk), 1)
    cur = kj <= qi
    dist = ((qi - kj) & (blk - 1)).astype(F32)
    dist = jnp.where(jnp.logical_and(n == 0, kj > qi), jnp.inf, dist)
    cur_b = jnp.where(cur, 1.0, 0.0).astype(BF)
    prev_b = jnp.where(cur, 0.0, 1.0).astype(BF)
    lane = lax.broadcasted_iota(jnp.int32, (2 * blk, 2 * dh), 1)
    half_b = (jnp.where(lane < dh, 1.0, 0.0).astype(BF), jnp.where(lane < dh, 0.0, 1.0).astype(BF))
    qscale = jnp.asarray(dh ** -0.5, BF)
    nt = (((1,), (1,)), ((), ()))
    for p in range(ATT_KV_HEADS // 2):
        col = slice(p * 2 * dh, (p + 1) * 2 * dh)
        kcat = jnp.concatenate([kp_ref[:, col], kc_ref[:, col]], axis=0)
        vcat = jnp.concatenate([vp_ref[:, col], vc_ref[:, col]], axis=0)
        k_side = [kcat * hb for hb in half_b]
        v_side = [vcat * hb for hb in half_b]
        for g in range(group):
            c = p * group + g
            qcol = slice(c * 2 * dh, (c + 1) * 2 * dh)
            q2 = q_ref[:, qcol] * qscale
            acc = None
            for side in range(2):
                h = (2 * p + side) * group + g
                slope = 2.0 ** (-8.0 * (h + 1) / ATT_HEADS)
                s = lax.dot_general(q2, k_side[side], nt, preferred_element_type=F32)
                t = jnp.where(cur, s[:, blk:], s[:, :blk]) - slope * dist
                sink = sink_ref[h]
                m = jnp.maximum(jnp.max(t, axis=-1, keepdims=True), sink)
                pe = jnp.exp(t - m)
                den = jnp.sum(pe, axis=-1, keepdims=True) + jnp.exp(sink - m)
                pb = pe.astype(BF)
                pcat = jnp.concatenate([pb * prev_b, pb * cur_b], axis=1)
                o = jnp.dot(pcat, v_side[side], preferred_element_type=F32) / den
                acc = o if acc is None else acc + o
            o_ref[:, qcol] = acc.astype(BF)


def _attn(proj, B, S, sinks):
    T = B * S
    nb = S // ATT_BLOCK
    cq = COL_Q // ATT_Q_DIM
    ck, cv = COL_K // ATT_KV_DIM, COL_V // ATT_KV_DIM
    cur = lambda c: (lambda b, n: (b * nb + n, c))
    prev = lambda c: (lambda b, n: (b * nb + jnp.maximum(n - 1, 0), c))
    return pl.pallas_call(
        _attn_body,
        grid=(B, nb),
        in_specs=[
            pl.BlockSpec(memory_space=pltpu.SMEM),
            pl.BlockSpec((ATT_BLOCK, ATT_Q_DIM), cur(cq)),
            pl.BlockSpec((ATT_BLOCK, ATT_KV_DIM), prev(ck)),
            pl.BlockSpec((ATT_BLOCK, ATT_KV_DIM), cur(ck)),
            pl.BlockSpec((ATT_BLOCK, ATT_KV_DIM), prev(cv)),
            pl.BlockSpec((ATT_BLOCK, ATT_KV_DIM), cur(cv)),
        ],
        out_specs=pl.BlockSpec((ATT_BLOCK, ATT_Q_DIM), lambda b, n: (b * nb + n, 0)),
        out_shape=jax.ShapeDtypeStruct((T, ATT_Q_DIM), BF),
        compiler_params=_params("parallel", "arbitrary"),
        name="attn",
    )(sinks, proj, proj, proj, proj, proj)


def _gla_body(q_ref, k_ref, v_ref, r_ref, la_ref, nw_ref, o_ref, st_ref, *, nchunk):
    ck, dk, dv = GLA_CHUNK, GLA_DK, GLA_DV
    lb = nchunk * ck

    @pl.when(pl.program_id(1) == 0)
    def _():
        st_ref[...] = jnp.zeros_like(st_ref)

    ri = lax.broadcasted_iota(jnp.int32, (lb, lb), 0)
    ci = lax.broadcasted_iota(jnp.int32, (lb, lb), 1)
    shift = ck.bit_length() - 1
    causal = jnp.logical_and(ci <= ri, jnp.right_shift(ri, shift) == jnp.right_shift(ci, shift))
    tril = jnp.where(causal, 1.0, 0.0).astype(BF)
    nt = (((1,), (1,)), ((), ()))
    tn = (((0,), (0,)), ((), ()))
    qscale = dk ** -0.5

    la = la_ref[...]
    hi = la.astype(BF)
    r1 = la - hi.astype(F32)
    mid = r1.astype(BF)
    lo = (r1 - mid.astype(F32)).astype(BF)
    b = (jnp.dot(tril, hi, preferred_element_type=F32)
         + jnp.dot(tril, mid, preferred_element_type=F32)
         + jnp.dot(tril, lo, preferred_element_type=F32))
    chunks = [slice(c * ck, (c + 1) * ck) for c in range(nchunk)]
    b_last = [b[(c + 1) * ck - 1:(c + 1) * ck, :] for c in range(nchunk)]
    eb = jnp.exp(b)
    enb = jnp.exp(-b)
    ee = jnp.concatenate([jnp.exp(b_last[c] - b[chunks[c], :]) for c in range(nchunk)], axis=0)
    dec = [jnp.exp(b_last[c]) for c in range(nchunk)]
    for h in range(GLA_HEADS):
        sk = slice(h * dk, (h + 1) * dk)
        sv = slice(h * dv, (h + 1) * dv)
        q = q_ref[:, sk].astype(F32) * qscale
        k = k_ref[:, sk].astype(F32)
        v = v_ref[:, sv]
        qt = (q * eb[:, sk]).astype(BF)
        kt = (k * enb[:, sk]).astype(BF)
        ke = (k * ee[:, sk]).astype(BF)
        a = lax.dot_general(qt, kt, nt, preferred_element_type=F32)
        a = jnp.where(causal, a, 0.0).astype(BF)
        o = jnp.dot(a, v, preferred_element_type=F32)
        st = st_ref[h]
        inter = []
        for c in range(nchunk):
            inter.append(lax.dot_general(qt[chunks[c], :], st.astype(BF), nt, preferred_element_type=F32))
            st = st * dec[c][:, sk] + lax.dot_general(v[chunks[c], :], ke[chunks[c], :], tn,
                                                      preferred_element_type=F32)
        st_ref[h] = st
        o = o + jnp.concatenate(inter, axis=0)
        on = o * lax.rsqrt(jnp.mean(o * o, axis=-1, keepdims=True) + EPS)
        r = r_ref[:, sv].astype(F32)
        o_ref[:, sv] = (on * nw_ref[:, sv] * (r * jax.nn.sigmoid(r))).astype(BF)


def _gla(proj, la, B, S, norm_w):
    T = B * S
    lb = min(GLA_LB, S)
    nS = S // lb
    row = lambda c: (lambda b, s: (b * nS + s, c))
    return pl.pallas_call(
        functools.partial(_gla_body, nchunk=lb // GLA_CHUNK),
        grid=(B, nS),
        in_specs=[
            pl.BlockSpec((lb, GLA_K_DIM), row(COL_GQ // GLA_K_DIM)),
            pl.BlockSpec((lb, GLA_K_DIM), row(COL_GK // GLA_K_DIM)),
            pl.BlockSpec((lb, GLA_V_DIM), row(COL_GV // GLA_V_DIM)),
            pl.BlockSpec((lb, GLA_V_DIM), row(COL_GR // GLA_V_DIM)),
            pl.BlockSpec((lb, GLA_K_DIM), row(0)),
            pl.BlockSpec((1, GLA_V_DIM), lambda b, s: (0, 0)),
        ],
        out_specs=pl.BlockSpec((lb, GLA_V_DIM), row(0)),
        out_shape=jax.ShapeDtypeStruct((T, GLA_V_DIM), BF),
        scratch_shapes=[pltpu.VMEM((GLA_HEADS, GLA_DV, GLA_DK), F32)],
        compiler_params=_params("parallel", "arbitrary"),
        name="gla",
    )(proj, proj, proj, proj, la, norm_w.reshape(1, GLA_V_DIM))


def _merge_body(x_ref, ga_ref, gb_ref, gc_ref, ba_ref, bb_ref, bc_ref, uc_ref, ua_ref, ug_ref,
                wpw_ref, wao_ref, wgo_ref, wout_ref, o_ref):
    def gate(p_ref, b_ref):
        return jax.nn.sigmoid(p_ref[...].astype(F32) + b_ref[...])

    m = gate(ga_ref, ba_ref) * jnp.dot(uc_ref[...], wpw_ref[...], preferred_element_type=F32)
    m = m + gate(gb_ref, bb_ref) * jnp.dot(ua_ref[...], wao_ref[...], preferred_element_type=F32)
    m = m + gate(gc_ref, bc_ref) * jnp.dot(ug_ref[...], wgo_ref[...], preferred_element_type=F32)
    o_ref[...] = x_ref[...] + jnp.dot(m.astype(BF), wout_ref[...], preferred_element_type=F32)


def _merge(x, proj, b_gate, uc, ua, ug, w_pw, w_ao, w_go, w_out):
    T, D = x.shape
    tm = min(MERGE_TM, T)
    g0 = COL_GATE // D
    const = lambda shape: pl.BlockSpec(shape, lambda i: (0, 0), pipeline_mode=pl.Buffered(1))
    act = lambda w, c: pl.BlockSpec((tm, w), lambda i: (i, c))
    bias = lambda c: pl.BlockSpec((1, D), lambda i: (0, c))
    bg = b_gate.reshape(1, 3 * D)
    return pl.pallas_call(
        _merge_body,
        grid=(T // tm,),
        in_specs=[
            act(D, 0),
            act(D, g0), act(D, g0 + 1), act(D, g0 + 2),
            bias(0), bias(1), bias(2),
            act(CONV_CH, 0), act(ATT_Q_DIM, 0), act(GLA_V_DIM, 0),
            const((CONV_CH, D)), const((ATT_Q_DIM, D)), const((GLA_V_DIM, D)), const((D, D)),
        ],
        out_specs=act(D, 0),
        out_shape=jax.ShapeDtypeStruct((T, D), F32),
        compiler_params=_params("parallel"),
        name="merge",
    )(x, proj, proj, proj, bg, bg, bg, uc, ua, ug, w_pw, w_ao, w_go, w_out)


def _pair_heads(w, axis):
    group = ATT_HEADS // ATT_KV_HEADS
    split = (ATT_KV_HEADS // 2, 2, group, ATT_HEAD_DIM)
    if axis == 0:
        return w.reshape(*split, w.shape[1]).transpose(0, 2, 1, 3, 4).reshape(w.shape)
    return w.reshape(w.shape[0], *split).transpose(0, 1, 3, 2, 4).reshape(w.shape)


def _regroup_body(w_ref, main_ref, wa_ref):
    c = 2 * CONV_CH
    o_q = c
    o_k = o_q + ATT_Q_DIM
    o_v = o_k + ATT_KV_DIM
    o_gq = o_v + ATT_KV_DIM
    o_gk = o_gq + GLA_K_DIM
    o_gv = o_gk + GLA_K_DIM
    o_gr = o_gv + GLA_V_DIM
    o_ga = o_gr + GLA_V_DIM
    o_gate = o_ga + GLA_RANK

    def put(dst, src, n):
        main_ref[dst:dst + n, :] = w_ref[src:src + n, :].astype(BF)

    put(COL_GATE, o_gate, 3 * D_MODEL)
    put(COL_CONV, 0, c)
    group, dh = ATT_HEADS // ATT_KV_HEADS, ATT_HEAD_DIM
    for col in range(ATT_HEADS // 2):
        p, g = divmod(col, group)
        for side in range(2):
            h = (2 * p + side) * group + g
            put(COL_Q + (2 * col + side) * dh, o_q + h * dh, dh)
    put(COL_GV, o_gv, GLA_V_DIM)
    put(COL_GR, o_gr, GLA_V_DIM)
    put(COL_GQ, o_gq, GLA_K_DIM)
    put(COL_GK, o_gk, GLA_K_DIM)
    put(COL_K, o_k, ATT_KV_DIM)
    put(COL_V, o_v, ATT_KV_DIM)
    cols = w_ref.shape[1]
    wa_ref[...] = jnp.concatenate(
        [w_ref[o_ga:o_gate, :], jnp.zeros((LANES - GLA_RANK, cols), F32)], axis=0).astype(BF)


def _regroup_w_in(w_in, layer):
    w_t = jnp.swapaxes(w_in, 1, 2)
    _, d_in, D = w_t.shape
    tc = REGROUP_COLS
    return pl.pallas_call(
        _regroup_body,
        grid=(D // tc,),
        in_specs=[pl.BlockSpec((None, d_in, tc), lambda i: (layer, 0, i))],
        out_specs=[pl.BlockSpec((N_PROJ, tc), lambda i: (0, i)),
                   pl.BlockSpec((LANES, tc), lambda i: (0, i))],
        out_shape=[jax.ShapeDtypeStruct((N_PROJ, D), BF), jax.ShapeDtypeStruct((LANES, D), BF)],
        compiler_params=_params("parallel"),
        name="regroup",
    )(w_t)


def kernel(x, ffn1_norm, ffn1_w_gate, ffn1_w_up, ffn1_w_down, mix_norm, w_in, b_gate, conv_dw_w, conv_dw_b, conv_ln_w, conv_ln_b, conv_w_pw, att_sinks, att_w_o, gla_w_alpha, gla_b_alpha, gla_norm_w, gla_w_o, w_out, ffn2_norm, ffn2_w_gate, ffn2_w_up, ffn2_w_down, final_norm):
    B, S, D = x.shape
    xt = x.reshape(B * S, D)
    for l in range(DEPTH):
        xt = _ffn(xt, ffn1_norm[l], ffn1_w_gate, ffn1_w_up, ffn1_w_down, l)
        w_main, w_a = _regroup_w_in(w_in, l)
        w_alpha = jnp.pad(gla_w_alpha[l], ((0, LANES - GLA_RANK), (0, 0))).astype(BF)
        proj, la = _inproj(xt, mix_norm[l], w_main, w_a, w_alpha, gla_b_alpha[l])
        uc = _conv(proj, B, S, conv_dw_w[l], conv_dw_b[l], conv_ln_w[l], conv_ln_b[l])
        ua = _attn(proj, B, S, att_sinks[l])
        ug = _gla(proj, la, B, S, gla_norm_w[l])
        xt = _merge(xt, proj, b_gate[l], uc, ua, ug, conv_w_pw[l].astype(BF),
                    _pair_heads(att_w_o[l], axis=0).astype(BF),
                    gla_w_o[l].astype(BF), w_out[l].astype(BF))
        xt = _ffn(xt, ffn2_norm[l], ffn2_w_gate, ffn2_w_up, ffn2_w_down, l,
                  final_w=final_norm if l == DEPTH - 1 else None)
    return xt.reshape(B, S, D)
```

```python
import functools

import jax
import jax.numpy as jnp
from jax import lax
from jax.experimental import pallas as pl
from jax.experimental.pallas import tpu as pltpu

BF = jnp.bfloat16
F32 = jnp.float32

D_MODEL = 2048
DEPTH = 2
CONV_CH = 1024
CONV_WIDTH = 31
ATT_HEADS = 16
ATT_KV_HEADS = 4
ATT_HEAD_DIM = 64
ATT_BLOCK = 128
ATT_Q_DIM = 1024
ATT_KV_DIM = 256
GLA_HEADS = 4
GLA_DK = 128
GLA_DV = 256
GLA_K_DIM = 512
GLA_V_DIM = 1024
GLA_RANK = 16
GLA_GATE_NORM = 16.0
GLA_CHUNK = 64
D_FF = 5632
EPS = 1e-6

VMEM_LIMIT_BYTES = 60 * 1024 * 1024
LANES = 128
SUBLANES = 8

N_PROJ = 12800
COL_GATE = 0
COL_CONV = 6144
COL_Q = 8192
COL_GV = 9216
COL_GR = 10240
COL_GQ = 11264
COL_GK = 11776
COL_K = 12288
COL_V = 12544

FFN_TM = 1024
FFN_TF = 512
PROJ_TM = 1024
PROJ_TN = 1280
CONV_TS = 512
CONV_HALO = 32
CONV_ROWS = 32
MIX_TM = 256
REGROUP_COLS = 256


def _params(*sem):
    return pltpu.CompilerParams(dimension_semantics=sem, vmem_limit_bytes=VMEM_LIMIT_BYTES)


def _rms(x):
    return x * lax.rsqrt(jnp.mean(x * x, axis=-1, keepdims=True) + EPS)


def _ffn_body(x_hbm, nw_ref, wg_hbm, wu_hbm, wd_hbm, *rest, final, layer, tm, tf, n_i, n_j):
    if final:
        fw_ref, o_hbm, acc_ref, h_ref, wg_buf, wu_buf, wd_buf, wsem, xsem, osem = rest
    else:
        o_hbm, acc_ref, h_ref, wg_buf, wu_buf, wd_buf, wsem, xsem, osem = rest
    i = pl.program_id(0)
    j = pl.program_id(1)
    step = i * n_j + j
    buf = i % 2
    slot = step % 2

    def weight_copies(jj, sl):
        cols = pl.ds(pl.multiple_of(jj * tf, tf), tf)
        return (pltpu.make_async_copy(wg_hbm.at[layer, :, cols], wg_buf.at[sl], wsem.at[sl, 0]),
                pltpu.make_async_copy(wu_hbm.at[layer, :, cols], wu_buf.at[sl], wsem.at[sl, 1]),
                pltpu.make_async_copy(wd_hbm.at[layer, cols, :], wd_buf.at[sl], wsem.at[sl, 2]))

    def tile_rows(ii):
        return pl.ds(pl.multiple_of(ii * tm, tm), tm)

    def x_copy(ii, b):
        return pltpu.make_async_copy(x_hbm.at[tile_rows(ii), :], acc_ref.at[b], xsem.at[b])

    def out_copy(ii, b):
        return pltpu.make_async_copy(acc_ref.at[b], o_hbm.at[tile_rows(ii), :], osem.at[b])

    @pl.when(step == 0)
    def _():
        x_copy(0, 0).start()
        for cp in weight_copies(0, 0):
            cp.start()

    @pl.when(step + 1 < n_i * n_j)
    def _():
        j_next = jnp.where(j + 1 == n_j, 0, j + 1)
        for cp in weight_copies(j_next, 1 - slot):
            cp.start()

    @pl.when(j == 0)
    def _():
        x_copy(i, buf).wait()
        h_ref[...] = (_rms(acc_ref[buf]) * nw_ref[...]).astype(BF)

    @pl.when(j == 1)
    def _():
        @pl.when(i >= 1)
        def _():
            out_copy(i - 1, 1 - buf).wait()

        @pl.when(i + 1 < n_i)
        def _():
            x_copy(i + 1, 1 - buf).start()

    for cp in weight_copies(j, slot):
        cp.wait()
    h = h_ref[...]
    g = jnp.dot(h, wg_buf[slot].astype(BF), preferred_element_type=F32)
    u = jnp.dot(h, wu_buf[slot].astype(BF), preferred_element_type=F32)
    a = (0.5 * (g * jax.nn.sigmoid(g) * u)).astype(BF)
    acc_ref[buf] += jnp.dot(a, wd_buf[slot].astype(BF), preferred_element_type=F32)

    @pl.when(j == n_j - 1)
    def _():
        if final:
            acc_ref[buf] = _rms(acc_ref[buf]) * fw_ref[...]
        out_copy(i, buf).start()

    @pl.when(step == n_i * n_j - 1)
    def _():
        out_copy(i, buf).wait()


def _ffn(x, nw, wg, wu, wd, layer, final_w=None):
    T, D = x.shape
    F = wg.shape[2]
    tm, tf = min(FFN_TM, T), FFN_TF
    n_i, n_j = T // tm, F // tf
    assert n_j >= 2, "the buffer hand-over happens at d_ff step 1"
    final = final_w is not None
    hbm = pl.BlockSpec(memory_space=pl.ANY)
    vec = pl.BlockSpec((1, D), lambda i, j: (0, 0))
    in_specs = [hbm, vec, hbm, hbm, hbm]
    args = [x, nw.reshape(1, D), wg, wu, wd]
    if final:
        in_specs.append(vec)
        args.append(final_w.reshape(1, D))
    return pl.pallas_call(
        functools.partial(_ffn_body, final=final, layer=layer, tm=tm, tf=tf, n_i=n_i, n_j=n_j),
        grid=(n_i, n_j),
        in_specs=in_specs,
        out_specs=hbm,
        out_shape=jax.ShapeDtypeStruct((T, D), F32),
        scratch_shapes=[
            pltpu.VMEM((2, tm, D), F32),
            pltpu.VMEM((tm, D), BF),
            pltpu.VMEM((2, D, tf), F32),
            pltpu.VMEM((2, D, tf), F32),
            pltpu.VMEM((2, tf, D), F32),
            pltpu.SemaphoreType.DMA((2, 3)),
            pltpu.SemaphoreType.DMA((2,)),
            pltpu.SemaphoreType.DMA((2,)),
        ],
        compiler_params=_params("arbitrary", "arbitrary"),
        name="ffn",
    )(*args)


def _inproj_body(x_ref, nw_ref, w_ref, wa_ref, wal_ref, bal_ref, p_ref, la_ref, h_ref):
    j = pl.program_id(1)
    nt = (((1,), (1,)), ((), ()))

    @pl.when(j == 0)
    def _():
        h = (_rms(x_ref[...]) * nw_ref[...]).astype(BF)
        h_ref[...] = h
        a_low = lax.dot_general(h, wa_ref[...], nt, preferred_element_type=F32)
        z = jnp.dot(a_low.astype(BF), wal_ref[...], preferred_element_type=F32) + bal_ref[...]
        la_ref[...] = (jnp.minimum(z, 0.0) - jnp.log1p(jnp.exp(-jnp.abs(z)))) / GLA_GATE_NORM

    p_ref[...] = lax.dot_general(h_ref[...], w_ref[...], nt, preferred_element_type=F32).astype(BF)


def _inproj(x, nw, w, wa, wal, bal):
    T, D = x.shape
    tm, tn = min(PROJ_TM, T), PROJ_TN
    return pl.pallas_call(
        _inproj_body,
        grid=(T // tm, N_PROJ // tn),
        in_specs=[
            pl.BlockSpec((tm, D), lambda i, j: (i, 0)),
            pl.BlockSpec((1, D), lambda i, j: (0, 0)),
            pl.BlockSpec((tn, D), lambda i, j: (j, 0)),
            pl.BlockSpec((LANES, D), lambda i, j: (0, 0)),
            pl.BlockSpec((LANES, GLA_K_DIM), lambda i, j: (0, 0)),
            pl.BlockSpec((1, GLA_K_DIM), lambda i, j: (0, 0)),
        ],
        out_specs=[
            pl.BlockSpec((tm, tn), lambda i, j: (i, j)),
            pl.BlockSpec((tm, GLA_K_DIM), lambda i, j: (i, 0)),
        ],
        out_shape=[
            jax.ShapeDtypeStruct((T, N_PROJ), BF),
            jax.ShapeDtypeStruct((T, GLA_K_DIM), F32),
        ],
        scratch_shapes=[pltpu.VMEM((tm, D), BF)],
        compiler_params=_params("parallel", "arbitrary"),
        name="inproj",
    )(x, nw.reshape(1, D), w, wa, wal, bal.reshape(1, GLA_K_DIM))


def _conv_body(a_ref, g_ref, w_ref, b_ref, lnw_ref, lnb_ref, o_ref, zs_ref, wb_ref, y_ref, *, ts):
    s = pl.program_id(1)
    halo, rows = CONV_HALO, CONV_ROWS

    @pl.when(s == 0)
    def _():
        zs_ref[:, 0:halo, :] = jnp.zeros((SUBLANES, halo, CONV_CH), F32)
        for j in range(CONV_WIDTH):
            wb_ref[j] = jnp.broadcast_to(w_ref[j:j + 1, :], (SUBLANES, CONV_CH))

    @pl.when(s > 0)
    def _():
        zs_ref[:, 0:halo, :] = zs_ref[:, ts:ts + halo, :]

    a = a_ref[...].astype(F32)
    g = g_ref[...].astype(F32)
    zs_ref[0, halo:halo + ts, :] = a * jax.nn.sigmoid(g)

    def chunk(c, carry):
        r0 = pl.multiple_of(c * rows, rows)
        win = zs_ref[0, pl.ds(halo - SUBLANES + r0, rows + SUBLANES), :]
        for p in range(1, SUBLANES):
            zs_ref[p, pl.ds(halo + r0, rows), :] = win[SUBLANES - p:SUBLANES - p + rows, :]
        accs = [jnp.zeros((SUBLANES, CONV_CH), F32) + b_ref[...] for _ in range(rows // SUBLANES)]
        for d in range(CONV_WIDTH):
            wj = wb_ref[CONV_WIDTH - 1 - d]
            base = halo + r0 - SUBLANES * (d // SUBLANES)
            for i in range(rows // SUBLANES):
                zd = zs_ref[d % SUBLANES, pl.ds(base + i * SUBLANES, SUBLANES), :]
                accs[i] = accs[i] + wj * zd
        for i in range(rows // SUBLANES):
            y_ref[pl.ds(r0 + i * SUBLANES, SUBLANES), :] = accs[i]
        return carry

    lax.fori_loop(0, ts // rows, chunk, 0)

    y = y_ref[...]
    mu = jnp.mean(y, axis=-1, keepdims=True)
    dlt = y - mu
    var = jnp.mean(dlt * dlt, axis=-1, keepdims=True)
    zn = dlt * lax.rsqrt(var + EPS) * lnw_ref[...] + lnb_ref[...]
    o_ref[...] = (zn * jax.nn.sigmoid(zn)).astype(BF)


def _conv(proj, B, S, dw_w, dw_b, ln_w, ln_b):
    T = B * S
    ts = min(CONV_TS, S)
    nS = S // ts
    ca, cg = COL_CONV // CONV_CH, COL_CONV // CONV_CH + 1
    vec = pl.BlockSpec((1, CONV_CH), lambda b, s: (0, 0))
    return pl.pallas_call(
        functools.partial(_conv_body, ts=ts),
        grid=(B, nS),
        in_specs=[
            pl.BlockSpec((ts, CONV_CH), lambda b, s: (b * nS + s, ca)),
            pl.BlockSpec((ts, CONV_CH), lambda b, s: (b * nS + s, cg)),
            pl.BlockSpec((CONV_WIDTH, CONV_CH), lambda b, s: (0, 0)),
            vec, vec, vec,
        ],
        out_specs=pl.BlockSpec((ts, CONV_CH), lambda b, s: (b * nS + s, 0)),
        out_shape=jax.ShapeDtypeStruct((T, CONV_CH), BF),
        scratch_shapes=[pltpu.VMEM((SUBLANES, ts + CONV_HALO, CONV_CH), F32),
                        pltpu.VMEM((CONV_WIDTH, SUBLANES, CONV_CH), F32),
                        pltpu.VMEM((ts, CONV_CH), F32)],
        compiler_params=_params("parallel", "arbitrary"),
        name="conv",
    )(proj, proj, dw_w.reshape(CONV_WIDTH, CONV_CH), dw_b.reshape(1, CONV_CH),
      ln_w.reshape(1, CONV_CH), ln_b.reshape(1, CONV_CH))


def _attn_tile(first, sink_ref, q_ref, kp_ref, kc_ref, vp_ref, vc_ref, o_ref, nblk):
    blk, dh = ATT_BLOCK, ATT_HEAD_DIM
    group = ATT_HEADS // ATT_KV_HEADS
    qi = lax.broadcasted_iota(jnp.int32, (blk, blk), 0)
    kj = lax.broadcasted_iota(jnp.int32, (blk, blk), 1)
    cur = kj <= qi
    dist_in = ((qi - kj) & (blk - 1)).astype(F32)
    dist_first = jnp.where(jnp.logical_and(first, kj > qi), jnp.inf, dist_in)
    cur_b = jnp.where(cur, 1.0, 0.0).astype(BF)
    prev_b = jnp.where(cur, 0.0, 1.0).astype(BF)
    lane = lax.broadcasted_iota(jnp.int32, (2 * blk, 2 * dh), 1)
    half_b = (jnp.where(lane < dh, 1.0, 0.0).astype(BF), jnp.where(lane < dh, 0.0, 1.0).astype(BF))
    qscale = jnp.asarray(dh ** -0.5, BF)
    nt = (((1,), (1,)), ((), ()))
    for n in range(nblk):
        rows = slice(n * blk, (n + 1) * blk)
        before = slice((n - 1) * blk, n * blk)
        dist = dist_first if n == 0 else dist_in
        for p in range(ATT_KV_HEADS // 2):
            col = slice(p * 2 * dh, (p + 1) * 2 * dh)
            k_prev = kp_ref[:, col] if n == 0 else kc_ref[before, col]
            v_prev = vp_ref[:, col] if n == 0 else vc_ref[before, col]
            kcat = jnp.concatenate([k_prev, kc_ref[rows, col]], axis=0)
            vcat = jnp.concatenate([v_prev, vc_ref[rows, col]], axis=0)
            k_side = [kcat * hb for hb in half_b]
            v_side = [vcat * hb for hb in half_b]
            for g in range(group):
                c = p * group + g
                qcol = slice(c * 2 * dh, (c + 1) * 2 * dh)
                q2 = q_ref[rows, qcol] * qscale
                acc = None
                for side in range(2):
                    h = (2 * p + side) * group + g
                    slope = 2.0 ** (-8.0 * (h + 1) / ATT_HEADS)
                    s = lax.dot_general(q2, k_side[side], nt, preferred_element_type=F32)
                    t = jnp.where(cur, s[:, blk:], s[:, :blk]) - slope * dist
                    sink = sink_ref[h]
                    m = jnp.maximum(jnp.max(t, axis=-1, keepdims=True), sink)
                    pe = jnp.exp(t - m)
                    den = jnp.sum(pe, axis=-1, keepdims=True) + jnp.exp(sink - m)
                    pb = pe.astype(BF)
                    pcat = jnp.concatenate([pb * prev_b, pb * cur_b], axis=1)
                    o = jnp.dot(pcat, v_side[side], preferred_element_type=F32) / den
                    acc = o if acc is None else acc + o
                o_ref[rows, qcol] = acc.astype(BF)


def _gla_block(q_ref, k_ref, v_ref, r_ref, la_ref, nw_ref, o_ref, st_ref, nchunk):
    ck, dk, dv = GLA_CHUNK, GLA_DK, GLA_DV
    lb = nchunk * ck
    ri = lax.broadcasted_iota(jnp.int32, (lb, lb), 0)
    ci = lax.broadcasted_iota(jnp.int32, (lb, lb), 1)
    shift = ck.bit_length() - 1
    causal = jnp.logical_and(ci <= ri, jnp.right_shift(ri, shift) == jnp.right_shift(ci, shift))
    tril = jnp.where(causal, 1.0, 0.0).astype(BF)
    nt = (((1,), (1,)), ((), ()))
    tn = (((0,), (0,)), ((), ()))
    qscale = dk ** -0.5

    la = la_ref[...]
    hi = la.astype(BF)
    r1 = la - hi.astype(F32)
    mid = r1.astype(BF)
    lo = (r1 - mid.astype(F32)).astype(BF)
    b = (jnp.dot(tril, hi, preferred_element_type=F32)
         + jnp.dot(tril, mid, preferred_element_type=F32)
         + jnp.dot(tril, lo, preferred_element_type=F32))
    chunks = [slice(c * ck, (c + 1) * ck) for c in range(nchunk)]
    b_last = [b[(c + 1) * ck - 1:(c + 1) * ck, :] for c in range(nchunk)]
    eb = jnp.exp(b)
    enb = jnp.exp(-b)
    ee = jnp.concatenate([jnp.exp(b_last[c] - b[chunks[c], :]) for c in range(nchunk)], axis=0)
    dec = [jnp.exp(b_last[c]) for c in range(nchunk)]
    for h in range(GLA_HEADS):
        sk = slice(h * dk, (h + 1) * dk)
        sv = slice(h * dv, (h + 1) * dv)
        q = q_ref[:, sk].astype(F32) * qscale
        k = k_ref[:, sk].astype(F32)
        v = v_ref[:, sv]
        qt = (q * eb[:, sk]).astype(BF)
        kt = (k * enb[:, sk]).astype(BF)
        ke = (k * ee[:, sk]).astype(BF)
        a = lax.dot_general(qt, kt, nt, preferred_element_type=F32)
        a = jnp.where(causal, a, 0.0).astype(BF)
        o = jnp.dot(a, v, preferred_element_type=F32)
        st = st_ref[h]
        inter = []
        for c in range(nchunk):
            inter.append(lax.dot_general(qt[chunks[c], :], st.astype(BF), nt, preferred_element_type=F32))
            st = st * dec[c][:, sk] + lax.dot_general(v[chunks[c], :], ke[chunks[c], :], tn,
                                                      preferred_element_type=F32)
        st_ref[h] = st
        o = o + jnp.concatenate(inter, axis=0)
        on = o * lax.rsqrt(jnp.mean(o * o, axis=-1, keepdims=True) + EPS)
        r = r_ref[:, sv].astype(F32)
        o_ref[:, sv] = (on * nw_ref[:, sv] * (r * jax.nn.sigmoid(r))).astype(BF)


def _mix_body(sink_ref, x_ref, ga_ref, gb_ref, gc_ref, ba_ref, bb_ref, bc_ref, uc_ref,
              q_ref, kp_ref, kc_ref, vp_ref, vc_ref, gq_ref, gk_ref, gv_ref, gr_ref, la_ref, nw_ref,
              wpw_ref, wao_ref, wgo_ref, wout_ref, o_ref, ua_ref, ug_ref, st_ref, *, tiles_per_seq):
    first = pl.program_id(0) % tiles_per_seq == 0

    @pl.when(first)
    def _():
        st_ref[...] = jnp.zeros_like(st_ref)

    tm = x_ref.shape[0]
    _attn_tile(first, sink_ref, q_ref, kp_ref, kc_ref, vp_ref, vc_ref, ua_ref, tm // ATT_BLOCK)
    _gla_block(gq_ref, gk_ref, gv_ref, gr_ref, la_ref, nw_ref, ug_ref, st_ref, tm // GLA_CHUNK)

    def gate(p_ref, b_ref):
        return jax.nn.sigmoid(p_ref[...].astype(F32) + b_ref[...])

    m = gate(ga_ref, ba_ref) * jnp.dot(uc_ref[...], wpw_ref[...], preferred_element_type=F32)
    m = m + gate(gb_ref, bb_ref) * jnp.dot(ua_ref[...], wao_ref[...], preferred_element_type=F32)
    m = m + gate(gc_ref, bc_ref) * jnp.dot(ug_ref[...], wgo_ref[...], preferred_element_type=F32)
    o_ref[...] = x_ref[...] + jnp.dot(m.astype(BF), wout_ref[...], preferred_element_type=F32)


def _mix(x, proj, la, B, S, b_gate, uc, sinks, gla_norm_w, w_pw, w_ao, w_go, w_out):
    T, D = x.shape
    tm = MIX_TM
    tiles_per_seq = S // tm
    per_tile = tm // ATT_BLOCK
    const = lambda shape: pl.BlockSpec(shape, lambda i: (0, 0), pipeline_mode=pl.Buffered(1))
    act = lambda w, off: pl.BlockSpec((tm, w), lambda i: (i, off // w))
    before = lambda off: pl.BlockSpec(
        (ATT_BLOCK, ATT_KV_DIM), lambda i: (jnp.maximum(i * per_tile - 1, 0), off // ATT_KV_DIM))
    bias = lambda c: pl.BlockSpec((1, D), lambda i: (0, c))
    bg = b_gate.reshape(1, 3 * D)
    return pl.pallas_call(
        functools.partial(_mix_body, tiles_per_seq=tiles_per_seq),
        grid=(T // tm,),
        in_specs=[
            pl.BlockSpec(memory_space=pltpu.SMEM),
            act(D, 0),
            act(D, COL_GATE), act(D, COL_GATE + D), act(D, COL_GATE + 2 * D),
            bias(0), bias(1), bias(2),
            act(CONV_CH, 0),
            act(ATT_Q_DIM, COL_Q),
            before(COL_K), act(ATT_KV_DIM, COL_K), before(COL_V), act(ATT_KV_DIM, COL_V),
            act(GLA_K_DIM, COL_GQ), act(GLA_K_DIM, COL_GK), act(GLA_V_DIM, COL_GV), act(GLA_V_DIM, COL_GR),
            act(GLA_K_DIM, 0),
            pl.BlockSpec((1, GLA_V_DIM), lambda i: (0, 0)),
            const((CONV_CH, D)), const((ATT_Q_DIM, D)), const((GLA_V_DIM, D)), const((D, D)),
        ],
        out_specs=act(D, 0),
        out_shape=jax.ShapeDtypeStruct((T, D), F32),
        scratch_shapes=[pltpu.VMEM((tm, ATT_Q_DIM), BF), pltpu.VMEM((tm, GLA_V_DIM), BF),
                        pltpu.VMEM((GLA_HEADS, GLA_DV, GLA_DK), F32)],
        compiler_params=_params("arbitrary"),
        name="mix",
    )(sinks, x, proj, proj, proj, bg, bg, bg, uc, proj, proj, proj, proj, proj,
      proj, proj, proj, proj, la, gla_norm_w.reshape(1, GLA_V_DIM), w_pw, w_ao, w_go, w_out)


def _pair_heads(w, axis):
    group = ATT_HEADS // ATT_KV_HEADS
    split = (ATT_KV_HEADS // 2, 2, group, ATT_HEAD_DIM)
    if axis == 0:
        return w.reshape(*split, w.shape[1]).transpose(0, 2, 1, 3, 4).reshape(w.shape)
    return w.reshape(w.shape[0], *split).transpose(0, 1, 3, 2, 4).reshape(w.shape)


def _regroup_body(w_ref, main_ref, wa_ref):
    c = 2 * CONV_CH
    o_q = c
    o_k = o_q + ATT_Q_DIM
    o_v = o_k + ATT_KV_DIM
    o_gq = o_v + ATT_KV_DIM
    o_gk = o_gq + GLA_K_DIM
    o_gv = o_gk + GLA_K_DIM
    o_gr = o_gv + GLA_V_DIM
    o_ga = o_gr + GLA_V_DIM
    o_gate = o_ga + GLA_RANK

    def put(dst, src, n):
        main_ref[dst:dst + n, :] = w_ref[src:src + n, :].astype(BF)

    put(COL_GATE, o_gate, 3 * D_MODEL)
    put(COL_CONV, 0, c)
    group, dh = ATT_HEADS // ATT_KV_HEADS, ATT_HEAD_DIM
    for col in range(ATT_HEADS // 2):
        p, g = divmod(col, group)
        for side in range(2):
            h = (2 * p + side) * group + g
            put(COL_Q + (2 * col + side) * dh, o_q + h * dh, dh)
    put(COL_GV, o_gv, GLA_V_DIM)
    put(COL_GR, o_gr, GLA_V_DIM)
    put(COL_GQ, o_gq, GLA_K_DIM)
    put(COL_GK, o_gk, GLA_K_DIM)
    put(COL_K, o_k, ATT_KV_DIM)
    put(COL_V, o_v, ATT_KV_DIM)
    cols = w_ref.shape[1]
    wa_ref[...] = jnp.concatenate(
        [w_ref[o_ga:o_gate, :], jnp.zeros((LANES - GLA_RANK, cols), F32)], axis=0).astype(BF)


def _regroup_w_in(w_in, layer):
    w_t = jnp.swapaxes(w_in, 1, 2)
    _, d_in, D = w_t.shape
    tc = REGROUP_COLS
    return pl.pallas_call(
        _regroup_body,
        grid=(D // tc,),
        in_specs=[pl.BlockSpec((None, d_in, tc), lambda i: (layer, 0, i))],
        out_specs=[pl.BlockSpec((N_PROJ, tc), lambda i: (0, i)),
                   pl.BlockSpec((LANES, tc), lambda i: (0, i))],
        out_shape=[jax.ShapeDtypeStruct((N_PROJ, D), BF), jax.ShapeDtypeStruct((LANES, D), BF)],
        compiler_params=_params("parallel"),
        name="regroup",
    )(w_t)


def kernel(x, ffn1_norm, ffn1_w_gate, ffn1_w_up, ffn1_w_down, mix_norm, w_in, b_gate, conv_dw_w, conv_dw_b, conv_ln_w, conv_ln_b, conv_w_pw, att_sinks, att_w_o, gla_w_alpha, gla_b_alpha, gla_norm_w, gla_w_o, w_out, ffn2_norm, ffn2_w_gate, ffn2_w_up, ffn2_w_down, final_norm):
    B, S, D = x.shape
    xt = x.reshape(B * S, D)
    for l in range(DEPTH):
        xt = _ffn(xt, ffn1_norm[l], ffn1_w_gate, ffn1_w_up, ffn1_w_down, l)
        w_main, w_a = _regroup_w_in(w_in, l)
        w_alpha = jnp.pad(gla_w_alpha[l], ((0, LANES - GLA_RANK), (0, 0))).astype(BF)
        proj, la = _inproj(xt, mix_norm[l], w_main, w_a, w_alpha, gla_b_alpha[l])
        uc = _conv(proj, B, S, conv_dw_w[l], conv_dw_b[l], conv_ln_w[l], conv_ln_b[l])
        xt = _mix(xt, proj, la, B, S, b_gate[l], uc, att_sinks[l], gla_norm_w[l], conv_w_pw[l].astype(BF),
                    _pair_heads(att_w_o[l], axis=0).astype(BF),
                    gla_w_o[l].astype(BF), w_out[l].astype(BF))
        xt = _ffn(xt, ffn2_norm[l], ffn2_w_gate, ffn2_w_up, ffn2_w_down, l,
                  final_w=final_norm if l == DEPTH - 1 else None)
    return xt.reshape(B, S, D)
```

```python
import functools

import jax
import jax.numpy as jnp
from jax import lax
from jax.experimental import pallas as pl
from jax.experimental.pallas import tpu as pltpu

BF = jnp.bfloat16
F32 = jnp.float32

D_MODEL = 2048
DEPTH = 2
CONV_CH = 1024
CONV_WIDTH = 31
ATT_HEADS = 16
ATT_KV_HEADS = 4
ATT_HEAD_DIM = 64
ATT_BLOCK = 128
ATT_Q_DIM = 1024
ATT_KV_DIM = 256
GLA_HEADS = 4
GLA_DK = 128
GLA_DV = 256
GLA_K_DIM = 512
GLA_V_DIM = 1024
GLA_RANK = 16
GLA_GATE_NORM = 16.0
GLA_CHUNK = 64
D_FF = 5632
EPS = 1e-6

VMEM_LIMIT_BYTES = 60 * 1024 * 1024
LANES = 128
SUBLANES = 8

N_PROJ = 12800
COL_GATE = 0
COL_CONV = 6144
COL_Q = 8192
COL_GV = 9216
COL_GR = 10240
COL_GQ = 11264
COL_GK = 11776
COL_K = 12288
COL_V = 12544

FFN_TM = 1024
FFN_TF = 512
PROJ_TM = 1024
PROJ_TN = 1280
CONV_TS = 512
CONV_HALO = 32
CONV_ROWS = 32
MIX_TM = 256
REGROUP_COLS = 256


def _params(*sem):
    return pltpu.CompilerParams(dimension_semantics=sem, vmem_limit_bytes=VMEM_LIMIT_BYTES)


def _rms(x):
    return x * lax.rsqrt(jnp.mean(x * x, axis=-1, keepdims=True) + EPS)


def _ffn_body(x_hbm, nw_ref, wg_hbm, wu_hbm, wd_hbm, *rest, final, layer, tm, tf, n_i, n_j):
    if final:
        fw_ref, o_hbm, acc_ref, h_ref, wg_buf, wu_buf, wd_buf, wsem, xsem, osem = rest
    else:
        o_hbm, acc_ref, h_ref, wg_buf, wu_buf, wd_buf, wsem, xsem, osem = rest
    i = pl.program_id(0)
    j = pl.program_id(1)
    step = i * n_j + j
    buf = i % 2
    slot = step % 2

    def weight_copies(jj, sl):
        cols = pl.ds(pl.multiple_of(jj * tf, tf), tf)
        return (pltpu.make_async_copy(wg_hbm.at[layer, :, cols], wg_buf.at[sl], wsem.at[sl, 0]),
                pltpu.make_async_copy(wu_hbm.at[layer, :, cols], wu_buf.at[sl], wsem.at[sl, 1]),
                pltpu.make_async_copy(wd_hbm.at[layer, cols, :], wd_buf.at[sl], wsem.at[sl, 2]))

    def tile_rows(ii):
        return pl.ds(pl.multiple_of(ii * tm, tm), tm)

    def x_copy(ii, b):
        return pltpu.make_async_copy(x_hbm.at[tile_rows(ii), :], acc_ref.at[b], xsem.at[b])

    def out_copy(ii, b):
        return pltpu.make_async_copy(acc_ref.at[b], o_hbm.at[tile_rows(ii), :], osem.at[b])

    @pl.when(step == 0)
    def _():
        x_copy(0, 0).start()
        for cp in weight_copies(0, 0):
            cp.start()

    @pl.when(step + 1 < n_i * n_j)
    def _():
        j_next = jnp.where(j + 1 == n_j, 0, j + 1)
        for cp in weight_copies(j_next, 1 - slot):
            cp.start()

    @pl.when(j == 0)
    def _():
        x_copy(i, buf).wait()
        h_ref[...] = (_rms(acc_ref[buf]) * nw_ref[...]).astype(BF)

    @pl.when(j == 1)
    def _():
        @pl.when(i >= 1)
        def _():
            out_copy(i - 1, 1 - buf).wait()

        @pl.when(i + 1 < n_i)
        def _():
            x_copy(i + 1, 1 - buf).start()

    for cp in weight_copies(j, slot):
        cp.wait()
    h = h_ref[...]
    g = jnp.dot(h, wg_buf[slot].astype(BF), preferred_element_type=F32)
    u = jnp.dot(h, wu_buf[slot].astype(BF), preferred_element_type=F32)
    a = (0.5 * (g * jax.nn.sigmoid(g) * u)).astype(BF)
    acc_ref[buf] += jnp.dot(a, wd_buf[slot].astype(BF), preferred_element_type=F32)

    @pl.when(j == n_j - 1)
    def _():
        if final:
            acc_ref[buf] = _rms(acc_ref[buf]) * fw_ref[...]
        out_copy(i, buf).start()

    @pl.when(step == n_i * n_j - 1)
    def _():
        out_copy(i, buf).wait()


def _ffn(x, nw, wg, wu, wd, layer, final_w=None):
    T, D = x.shape
    F = wg.shape[2]
    tm, tf = min(FFN_TM, T), FFN_TF
    n_i, n_j = T // tm, F // tf
    assert n_j >= 2, "the buffer hand-over happens at d_ff step 1"
    final = final_w is not None
    hbm = pl.BlockSpec(memory_space=pl.ANY)
    vec = pl.BlockSpec((1, D), lambda i, j: (0, 0))
    in_specs = [hbm, vec, hbm, hbm, hbm]
    args = [x, nw.reshape(1, D), wg, wu, wd]
    if final:
        in_specs.append(vec)
        args.append(final_w.reshape(1, D))
    return pl.pallas_call(
        functools.partial(_ffn_body, final=final, layer=layer, tm=tm, tf=tf, n_i=n_i, n_j=n_j),
        grid=(n_i, n_j),
        in_specs=in_specs,
        out_specs=hbm,
        out_shape=jax.ShapeDtypeStruct((T, D), F32),
        scratch_shapes=[
            pltpu.VMEM((2, tm, D), F32),
            pltpu.VMEM((tm, D), BF),
            pltpu.VMEM((2, D, tf), F32),
            pltpu.VMEM((2, D, tf), F32),
            pltpu.VMEM((2, tf, D), F32),
            pltpu.SemaphoreType.DMA((2, 3)),
            pltpu.SemaphoreType.DMA((2,)),
            pltpu.SemaphoreType.DMA((2,)),
        ],
        compiler_params=_params("arbitrary", "arbitrary"),
        name="ffn",
    )(*args)


def _inproj_body(x_ref, nw_ref, w_hbm, wa_ref, wal_ref, bal_ref, p_ref, la_ref, h_ref, w_buf, wsem,
                 *, tn, n_i, n_j):
    i = pl.program_id(0)
    j = pl.program_id(1)
    step = i * n_j + j
    nt = (((1,), (1,)), ((), ()))

    def w_copy(jj, slot):
        rows = pl.ds(pl.multiple_of(jj * tn, tn), tn)
        return pltpu.make_async_copy(w_hbm.at[rows, :], w_buf.at[slot], wsem.at[slot])

    @pl.when(step == 0)
    def _():
        w_copy(0, 0).start()
        w_copy(1, 1).start()

    @pl.when(step + 2 < n_i * n_j)
    def _():
        j_ahead = jnp.where(j + 2 >= n_j, j + 2 - n_j, j + 2)
        w_copy(j_ahead, (step + 2) % 3).start()

    @pl.when(j == 0)
    def _():
        h = (_rms(x_ref[...]) * nw_ref[...]).astype(BF)
        h_ref[...] = h
        a_low = lax.dot_general(h, wa_ref[...], nt, preferred_element_type=F32)
        z = jnp.dot(a_low.astype(BF), wal_ref[...], preferred_element_type=F32) + bal_ref[...]
        la_ref[...] = (jnp.minimum(z, 0.0) - jnp.log1p(jnp.exp(-jnp.abs(z)))) / GLA_GATE_NORM

    slot = step % 3
    w_copy(j, slot).wait()
    p_ref[...] = lax.dot_general(h_ref[...], w_buf[slot], nt, preferred_element_type=F32).astype(BF)


def _inproj(x, nw, w, wa, wal, bal):
    T, D = x.shape
    tm, tn = min(PROJ_TM, T), PROJ_TN
    n_i, n_j = T // tm, N_PROJ // tn
    assert n_j >= 2, "the ring is primed with the first two weight tiles"
    return pl.pallas_call(
        functools.partial(_inproj_body, tn=tn, n_i=n_i, n_j=n_j),
        grid=(n_i, n_j),
        in_specs=[
            pl.BlockSpec((tm, D), lambda i, j: (i, 0)),
            pl.BlockSpec((1, D), lambda i, j: (0, 0)),
            pl.BlockSpec(memory_space=pl.ANY),
            pl.BlockSpec((LANES, D), lambda i, j: (0, 0)),
            pl.BlockSpec((LANES, GLA_K_DIM), lambda i, j: (0, 0)),
            pl.BlockSpec((1, GLA_K_DIM), lambda i, j: (0, 0)),
        ],
        out_specs=[
            pl.BlockSpec((tm, tn), lambda i, j: (i, j)),
            pl.BlockSpec((tm, GLA_K_DIM), lambda i, j: (i, 0)),
        ],
        out_shape=[
            jax.ShapeDtypeStruct((T, N_PROJ), BF),
            jax.ShapeDtypeStruct((T, GLA_K_DIM), F32),
        ],
        scratch_shapes=[pltpu.VMEM((tm, D), BF), pltpu.VMEM((3, tn, D), BF),
                        pltpu.SemaphoreType.DMA((3,))],
        compiler_params=_params("arbitrary", "arbitrary"),
        name="inproj",
    )(x, nw.reshape(1, D), w, wa, wal, bal.reshape(1, GLA_K_DIM))


def _conv_body(a_ref, g_ref, w_ref, b_ref, lnw_ref, lnb_ref, o_ref, zs_ref, wb_ref, y_ref, *, ts):
    s = pl.program_id(1)
    halo, rows = CONV_HALO, CONV_ROWS

    @pl.when(s == 0)
    def _():
        zs_ref[:, 0:halo, :] = jnp.zeros((SUBLANES, halo, CONV_CH), F32)
        for j in range(CONV_WIDTH):
            wb_ref[j] = jnp.broadcast_to(w_ref[j:j + 1, :], (SUBLANES, CONV_CH))

    @pl.when(s > 0)
    def _():
        zs_ref[:, 0:halo, :] = zs_ref[:, ts:ts + halo, :]

    a = a_ref[...].astype(F32)
    g = g_ref[...].astype(F32)
    zs_ref[0, halo:halo + ts, :] = a * jax.nn.sigmoid(g)

    def chunk(c, carry):
        r0 = pl.multiple_of(c * rows, rows)
        win = zs_ref[0, pl.ds(halo - SUBLANES + r0, rows + SUBLANES), :]
        for p in range(1, SUBLANES):
            zs_ref[p, pl.ds(halo + r0, rows), :] = win[SUBLANES - p:SUBLANES - p + rows, :]
        accs = [jnp.zeros((SUBLANES, CONV_CH), F32) + b_ref[...] for _ in range(rows // SUBLANES)]
        for d in range(CONV_WIDTH):
            wj = wb_ref[CONV_WIDTH - 1 - d]
            base = halo + r0 - SUBLANES * (d // SUBLANES)
            for i in range(rows // SUBLANES):
                zd = zs_ref[d % SUBLANES, pl.ds(base + i * SUBLANES, SUBLANES), :]
                accs[i] = accs[i] + wj * zd
        for i in range(rows // SUBLANES):
            y_ref[pl.ds(r0 + i * SUBLANES, SUBLANES), :] = accs[i]
        return carry

    lax.fori_loop(0, ts // rows, chunk, 0)

    y = y_ref[...]
    mu = jnp.mean(y, axis=-1, keepdims=True)
    dlt = y - mu
    var = jnp.mean(dlt * dlt, axis=-1, keepdims=True)
    zn = dlt * lax.rsqrt(var + EPS) * lnw_ref[...] + lnb_ref[...]
    o_ref[...] = (zn * jax.nn.sigmoid(zn)).astype(BF)


def _conv(proj, B, S, dw_w, dw_b, ln_w, ln_b):
    T = B * S
    ts = min(CONV_TS, S)
    nS = S // ts
    ca, cg = COL_CONV // CONV_CH, COL_CONV // CONV_CH + 1
    vec = pl.BlockSpec((1, CONV_CH), lambda b, s: (0, 0))
    return pl.pallas_call(
        functools.partial(_conv_body, ts=ts),
        grid=(B, nS),
        in_specs=[
            pl.BlockSpec((ts, CONV_CH), lambda b, s: (b * nS + s, ca)),
            pl.BlockSpec((ts, CONV_CH), lambda b, s: (b * nS + s, cg)),
            pl.BlockSpec((CONV_WIDTH, CONV_CH), lambda b, s: (0, 0)),
            vec, vec, vec,
        ],
        out_specs=pl.BlockSpec((ts, CONV_CH), lambda b, s: (b * nS + s, 0)),
        out_shape=jax.ShapeDtypeStruct((T, CONV_CH), BF),
        scratch_shapes=[pltpu.VMEM((SUBLANES, ts + CONV_HALO, CONV_CH), F32),
                        pltpu.VMEM((CONV_WIDTH, SUBLANES, CONV_CH), F32),
                        pltpu.VMEM((ts, CONV_CH), F32)],
        compiler_params=_params("parallel", "arbitrary"),
        name="conv",
    )(proj, proj, dw_w.reshape(CONV_WIDTH, CONV_CH), dw_b.reshape(1, CONV_CH),
      ln_w.reshape(1, CONV_CH), ln_b.reshape(1, CONV_CH))


def _attn_tile(first, sink_ref, q_ref, kp_ref, kc_ref, vp_ref, vc_ref, o_ref, nblk):
    blk, dh = ATT_BLOCK, ATT_HEAD_DIM
    group = ATT_HEADS // ATT_KV_HEADS
    qi = lax.broadcasted_iota(jnp.int32, (blk, blk), 0)
    kj = lax.broadcasted_iota(jnp.int32, (blk, blk), 1)
    cur = kj <= qi
    dist_in = ((qi - kj) & (blk - 1)).astype(F32)
    dist_first = jnp.where(jnp.logical_and(first, kj > qi), jnp.inf, dist_in)
    cur_b = jnp.where(cur, 1.0, 0.0).astype(BF)
    prev_b = jnp.where(cur, 0.0, 1.0).astype(BF)
    lane = lax.broadcasted_iota(jnp.int32, (2 * blk, 2 * dh), 1)
    half_b = (jnp.where(lane < dh, 1.0, 0.0).astype(BF), jnp.where(lane < dh, 0.0, 1.0).astype(BF))
    qscale = jnp.asarray(dh ** -0.5, BF)
    nt = (((1,), (1,)), ((), ()))
    for n in range(nblk):
        rows = slice(n * blk, (n + 1) * blk)
        before = slice((n - 1) * blk, n * blk)
        dist = dist_first if n == 0 else dist_in
        for p in range(ATT_KV_HEADS // 2):
            col = slice(p * 2 * dh, (p + 1) * 2 * dh)
            k_prev = kp_ref[:, col] if n == 0 else kc_ref[before, col]
            v_prev = vp_ref[:, col] if n == 0 else vc_ref[before, col]
            kcat = jnp.concatenate([k_prev, kc_ref[rows, col]], axis=0)
            vcat = jnp.concatenate([v_prev, vc_ref[rows, col]], axis=0)
            k_side = [kcat * hb for hb in half_b]
            v_side = [vcat * hb for hb in half_b]
            for g in range(group):
                c = p * group + g
                qcol = slice(c * 2 * dh, (c + 1) * 2 * dh)
                q2 = q_ref[rows, qcol] * qscale
                acc = None
                for side in range(2):
                    h = (2 * p + side) * group + g
                    slope = 2.0 ** (-8.0 * (h + 1) / ATT_HEADS)
                    s = lax.dot_general(q2, k_side[side], nt, preferred_element_type=F32)
                    t = jnp.where(cur, s[:, blk:], s[:, :blk]) - slope * dist
                    sink = sink_ref[h]
                    m = jnp.maximum(jnp.max(t, axis=-1, keepdims=True), sink)
                    pe = jnp.exp(t - m)
                    den = jnp.sum(pe, axis=-1, keepdims=True) + jnp.exp(sink - m)
                    pb = pe.astype(BF)
                    pcat = jnp.concatenate([pb * prev_b, pb * cur_b], axis=1)
                    o = jnp.dot(pcat, v_side[side], preferred_element_type=F32) / den
                    acc = o if acc is None else acc + o
                o_ref[rows, qcol] = acc.astype(BF)


def _gla_block(q_ref, k_ref, v_ref, r_ref, la_ref, nw_ref, o_ref, st_ref, nchunk):
    ck, dk, dv = GLA_CHUNK, GLA_DK, GLA_DV
    lb = nchunk * ck
    ri = lax.broadcasted_iota(jnp.int32, (lb, lb), 0)
    ci = lax.broadcasted_iota(jnp.int32, (lb, lb), 1)
    shift = ck.bit_length() - 1
    causal = jnp.logical_and(ci <= ri, jnp.right_shift(ri, shift) == jnp.right_shift(ci, shift))
    tril = jnp.where(causal, 1.0, 0.0).astype(BF)
    nt = (((1,), (1,)), ((), ()))
    tn = (((0,), (0,)), ((), ()))
    qscale = dk ** -0.5

    la = la_ref[...]
    hi = la.astype(BF)
    r1 = la - hi.astype(F32)
    mid = r1.astype(BF)
    lo = (r1 - mid.astype(F32)).astype(BF)
    b = (jnp.dot(tril, hi, preferred_element_type=F32)
         + jnp.dot(tril, mid, preferred_element_type=F32)
         + jnp.dot(tril, lo, preferred_element_type=F32))
    chunks = [slice(c * ck, (c + 1) * ck) for c in range(nchunk)]
    b_last = [b[(c + 1) * ck - 1:(c + 1) * ck, :] for c in range(nchunk)]
    eb = jnp.exp(b)
    enb = jnp.exp(-b)
    ee = jnp.concatenate([jnp.exp(b_last[c] - b[chunks[c], :]) for c in range(nchunk)], axis=0)
    dec = [jnp.exp(b_last[c]) for c in range(nchunk)]
    for h in range(GLA_HEADS):
        sk = slice(h * dk, (h + 1) * dk)
        sv = slice(h * dv, (h + 1) * dv)
        q = q_ref[:, sk].astype(F32) * qscale
        k = k_ref[:, sk].astype(F32)
        v = v_ref[:, sv]
        qt = (q * eb[:, sk]).astype(BF)
        kt = (k * enb[:, sk]).astype(BF)
        ke = (k * ee[:, sk]).astype(BF)
        a = lax.dot_general(qt, kt, nt, preferred_element_type=F32)
        a = jnp.where(causal, a, 0.0).astype(BF)
        o = jnp.dot(a, v, preferred_element_type=F32)
        st = st_ref[h]
        inter = []
        for c in range(nchunk):
            inter.append(lax.dot_general(qt[chunks[c], :], st.astype(BF), nt, preferred_element_type=F32))
            st = st * dec[c][:, sk] + lax.dot_general(v[chunks[c], :], ke[chunks[c], :], tn,
                                                      preferred_element_type=F32)
        st_ref[h] = st
        o = o + jnp.concatenate(inter, axis=0)
        on = o * lax.rsqrt(jnp.mean(o * o, axis=-1, keepdims=True) + EPS)
        r = r_ref[:, sv].astype(F32)
        o_ref[:, sv] = (on * nw_ref[:, sv] * (r * jax.nn.sigmoid(r))).astype(BF)


def _mix_body(sink_ref, x_ref, ga_ref, gb_ref, gc_ref, ba_ref, bb_ref, bc_ref, uc_ref,
              q_ref, kp_ref, kc_ref, vp_ref, vc_ref, gq_ref, gk_ref, gv_ref, gr_ref, la_ref, nw_ref,
              wpw_ref, wao_ref, wgo_ref, wout_ref, o_ref, ua_ref, ug_ref, st_ref, *, tiles_per_seq):
    first = pl.program_id(0) % tiles_per_seq == 0

    @pl.when(first)
    def _():
        st_ref[...] = jnp.zeros_like(st_ref)

    tm = x_ref.shape[0]
    _attn_tile(first, sink_ref, q_ref, kp_ref, kc_ref, vp_ref, vc_ref, ua_ref, tm // ATT_BLOCK)
    _gla_block(gq_ref, gk_ref, gv_ref, gr_ref, la_ref, nw_ref, ug_ref, st_ref, tm // GLA_CHUNK)

    def gate(p_ref, b_ref):
        return jax.nn.sigmoid(p_ref[...].astype(F32) + b_ref[...])

    m = gate(ga_ref, ba_ref) * jnp.dot(uc_ref[...], wpw_ref[...], preferred_element_type=F32)
    m = m + gate(gb_ref, bb_ref) * jnp.dot(ua_ref[...], wao_ref[...], preferred_element_type=F32)
    m = m + gate(gc_ref, bc_ref) * jnp.dot(ug_ref[...], wgo_ref[...], preferred_element_type=F32)
    o_ref[...] = x_ref[...] + jnp.dot(m.astype(BF), wout_ref[...], preferred_element_type=F32)


def _mix(x, proj, la, B, S, b_gate, uc, sinks, gla_norm_w, w_pw, w_ao, w_go, w_out):
    T, D = x.shape
    tm = MIX_TM
    tiles_per_seq = S // tm
    per_tile = tm // ATT_BLOCK
    const = lambda shape: pl.BlockSpec(shape, lambda i: (0, 0), pipeline_mode=pl.Buffered(1))
    act = lambda w, off: pl.BlockSpec((tm, w), lambda i: (i, off // w))
    before = lambda off: pl.BlockSpec(
        (ATT_BLOCK, ATT_KV_DIM), lambda i: (jnp.maximum(i * per_tile - 1, 0), off // ATT_KV_DIM))
    bias = lambda c: pl.BlockSpec((1, D), lambda i: (0, c))
    bg = b_gate.reshape(1, 3 * D)
    return pl.pallas_call(
        functools.partial(_mix_body, tiles_per_seq=tiles_per_seq),
        grid=(T // tm,),
        in_specs=[
            pl.BlockSpec(memory_space=pltpu.SMEM),
            act(D, 0),
            act(D, COL_GATE), act(D, COL_GATE + D), act(D, COL_GATE + 2 * D),
            bias(0), bias(1), bias(2),
            act(CONV_CH, 0),
            act(ATT_Q_DIM, COL_Q),
            before(COL_K), act(ATT_KV_DIM, COL_K), before(COL_V), act(ATT_KV_DIM, COL_V),
            act(GLA_K_DIM, COL_GQ), act(GLA_K_DIM, COL_GK), act(GLA_V_DIM, COL_GV), act(GLA_V_DIM, COL_GR),
            act(GLA_K_DIM, 0),
            pl.BlockSpec((1, GLA_V_DIM), lambda i: (0, 0)),
            const((CONV_CH, D)), const((ATT_Q_DIM, D)), const((GLA_V_DIM, D)), const((D, D)),
        ],
        out_specs=act(D, 0),
        out_shape=jax.ShapeDtypeStruct((T, D), F32),
        scratch_shapes=[pltpu.VMEM((tm, ATT_Q_DIM), BF), pltpu.VMEM((tm, GLA_V_DIM), BF),
                        pltpu.VMEM((GLA_HEADS, GLA_DV, GLA_DK), F32)],
        compiler_params=_params("arbitrary"),
        name="mix",
    )(sinks, x, proj, proj, proj, bg, bg, bg, uc, proj, proj, proj, proj, proj,
      proj, proj, proj, proj, la, gla_norm_w.reshape(1, GLA_V_DIM), w_pw, w_ao, w_go, w_out)


def _pair_heads(w, axis):
    group = ATT_HEADS // ATT_KV_HEADS
    split = (ATT_KV_HEADS // 2, 2, group, ATT_HEAD_DIM)
    if axis == 0:
        return w.reshape(*split, w.shape[1]).transpose(0, 2, 1, 3, 4).reshape(w.shape)
    return w.reshape(w.shape[0], *split).transpose(0, 1, 3, 2, 4).reshape(w.shape)


def _regroup_body(w_ref, main_ref, wa_ref):
    c = 2 * CONV_CH
    o_q = c
    o_k = o_q + ATT_Q_DIM
    o_v = o_k + ATT_KV_DIM
    o_gq = o_v + ATT_KV_DIM
    o_gk = o_gq + GLA_K_DIM
    o_gv = o_gk + GLA_K_DIM
    o_gr = o_gv + GLA_V_DIM
    o_ga = o_gr + GLA_V_DIM
    o_gate = o_ga + GLA_RANK

    def put(dst, src, n):
        main_ref[dst:dst + n, :] = w_ref[src:src + n, :].astype(BF)

    put(COL_GATE, o_gate, 3 * D_MODEL)
    put(COL_CONV, 0, c)
    group, dh = ATT_HEADS // ATT_KV_HEADS, ATT_HEAD_DIM
    for col in range(ATT_HEADS // 2):
        p, g = divmod(col, group)
        for side in range(2):
            h = (2 * p + side) * group + g
            put(COL_Q + (2 * col + side) * dh, o_q + h * dh, dh)
    put(COL_GV, o_gv, GLA_V_DIM)
    put(COL_GR, o_gr, GLA_V_DIM)
    put(COL_GQ, o_gq, GLA_K_DIM)
    put(COL_GK, o_gk, GLA_K_DIM)
    put(COL_K, o_k, ATT_KV_DIM)
    put(COL_V, o_v, ATT_KV_DIM)
    cols = w_ref.shape[1]
    wa_ref[...] = jnp.concatenate(
        [w_ref[o_ga:o_gate, :], jnp.zeros((LANES - GLA_RANK, cols), F32)], axis=0).astype(BF)


def _regroup_w_in(w_in, layer):
    w_t = jnp.swapaxes(w_in, 1, 2)
    _, d_in, D = w_t.shape
    tc = REGROUP_COLS
    return pl.pallas_call(
        _regroup_body,
        grid=(D // tc,),
        in_specs=[pl.BlockSpec((None, d_in, tc), lambda i: (layer, 0, i))],
        out_specs=[pl.BlockSpec((N_PROJ, tc), lambda i: (0, i)),
                   pl.BlockSpec((LANES, tc), lambda i: (0, i))],
        out_shape=[jax.ShapeDtypeStruct((N_PROJ, D), BF), jax.ShapeDtypeStruct((LANES, D), BF)],
        compiler_params=_params("parallel"),
        name="regroup",
    )(w_t)


def kernel(x, ffn1_norm, ffn1_w_gate, ffn1_w_up, ffn1_w_down, mix_norm, w_in, b_gate, conv_dw_w, conv_dw_b, conv_ln_w, conv_ln_b, conv_w_pw, att_sinks, att_w_o, gla_w_alpha, gla_b_alpha, gla_norm_w, gla_w_o, w_out, ffn2_norm, ffn2_w_gate, ffn2_w_up, ffn2_w_down, final_norm):
    B, S, D = x.shape
    xt = x.reshape(B * S, D)
    for l in range(DEPTH):
        xt = _ffn(xt, ffn1_norm[l], ffn1_w_gate, ffn1_w_up, ffn1_w_down, l)
        w_main, w_a = _regroup_w_in(w_in, l)
        w_alpha = jnp.pad(gla_w_alpha[l], ((0, LANES - GLA_RANK), (0, 0))).astype(BF)
        proj, la = _inproj(xt, mix_norm[l], w_main, w_a, w_alpha, gla_b_alpha[l])
        uc = _conv(proj, B, S, conv_dw_w[l], conv_dw_b[l], conv_ln_w[l], conv_ln_b[l])
        xt = _mix(xt, proj, la, B, S, b_gate[l], uc, att_sinks[l], gla_norm_w[l], conv_w_pw[l].astype(BF),
                    _pair_heads(att_w_o[l], axis=0).astype(BF),
                    gla_w_o[l].astype(BF), w_out[l].astype(BF))
        xt = _ffn(xt, ffn2_norm[l], ffn2_w_gate, ffn2_w_up, ffn2_w_down, l,
                  final_w=final_norm if l == DEPTH - 1 else None)
    return xt.reshape(B, S, D)
```

```python
import functools

import jax
import jax.numpy as jnp
from jax import lax
from jax.experimental import pallas as pl
from jax.experimental.pallas import tpu as pltpu

BF = jnp.bfloat16
F32 = jnp.float32

D_MODEL = 2048
DEPTH = 2
CONV_CH = 1024
CONV_WIDTH = 31
ATT_HEADS = 16
ATT_KV_HEADS = 4
ATT_HEAD_DIM = 64
ATT_BLOCK = 128
ATT_Q_DIM = 1024
ATT_KV_DIM = 256
GLA_HEADS = 4
GLA_DK = 128
GLA_DV = 256
GLA_K_DIM = 512
GLA_V_DIM = 1024
GLA_RANK = 16
GLA_GATE_NORM = 16.0
GLA_CHUNK = 64
D_FF = 5632
EPS = 1e-6

VMEM_LIMIT_BYTES = 60 * 1024 * 1024
LANES = 128
SUBLANES = 8

N_PROJ = 12800
COL_GATE = 0
COL_CONV = 6144
COL_Q = 8192
COL_GV = 9216
COL_GR = 10240
COL_GQ = 11264
COL_GK = 11776
COL_K = 12288
COL_V = 12544

FFN_TM = 1024
FFN_TF = 512
FFN_NORM_ROWS = 256
PROJ_TM = 1024
PROJ_TN = 1280
PROJ_NORM_ROWS = 256
CONV_TS = 512
CONV_HALO = 32
CONV_ROWS = 32
MIX_TM = 256
REGROUP_COLS = 256


def _params(*sem):
    return pltpu.CompilerParams(dimension_semantics=sem, vmem_limit_bytes=VMEM_LIMIT_BYTES)


def _rms(x):
    return x * lax.rsqrt(jnp.mean(x * x, axis=-1, keepdims=True) + EPS)


def _ffn_body(x_hbm, nw_ref, wg_hbm, wu_hbm, wd_hbm, *rest, final, layer, tm, tf, n_i, n_j):
    if final:
        fw_ref, o_hbm, acc_ref, h_ref, wg_buf, wu_buf, wd_buf, wsem, xsem, osem = rest
    else:
        o_hbm, acc_ref, h_ref, wg_buf, wu_buf, wd_buf, wsem, xsem, osem = rest
    i = pl.program_id(0)
    j = pl.program_id(1)
    step = i * n_j + j
    buf = i % 2
    slot = step % 2

    def weight_copies(jj, sl):
        cols = pl.ds(pl.multiple_of(jj * tf, tf), tf)
        return (pltpu.make_async_copy(wg_hbm.at[layer, :, cols], wg_buf.at[sl], wsem.at[sl, 0]),
                pltpu.make_async_copy(wu_hbm.at[layer, :, cols], wu_buf.at[sl], wsem.at[sl, 1]),
                pltpu.make_async_copy(wd_hbm.at[layer, cols, :], wd_buf.at[sl], wsem.at[sl, 2]))

    def tile_rows(ii):
        return pl.ds(pl.multiple_of(ii * tm, tm), tm)

    def x_copy(ii, b):
        return pltpu.make_async_copy(x_hbm.at[tile_rows(ii), :], acc_ref.at[b], xsem.at[b])

    def out_copy(ii, b):
        return pltpu.make_async_copy(acc_ref.at[b], o_hbm.at[tile_rows(ii), :], osem.at[b])

    @pl.when(step == 0)
    def _():
        x_copy(0, 0).start()
        for cp in weight_copies(0, 0):
            cp.start()

    @pl.when(step + 1 < n_i * n_j)
    def _():
        j_next = jnp.where(j + 1 == n_j, 0, j + 1)
        for cp in weight_copies(j_next, 1 - slot):
            cp.start()

    @pl.when(j == 1)
    def _():
        @pl.when(i >= 1)
        def _():
            out_copy(i - 1, 1 - buf).wait()

        @pl.when(i + 1 < n_i)
        def _():
            x_copy(i + 1, 1 - buf).start()

    for cp in weight_copies(j, slot):
        cp.wait()

    def swiglu_into_acc(rows, h, wg, wu, wd):
        g = jnp.dot(h, wg, preferred_element_type=F32)
        u = jnp.dot(h, wu, preferred_element_type=F32)
        a = (0.5 * (g * jax.nn.sigmoid(g) * u)).astype(BF)
        acc_ref[buf, rows, :] += jnp.dot(a, wd, preferred_element_type=F32)

    def bf16_weights():
        return wg_buf[slot].astype(BF), wu_buf[slot].astype(BF), wd_buf[slot].astype(BF)

    @pl.when(j == 0)
    def _():
        x_copy(i, buf).wait()
        weights = bf16_weights()
        chunk = min(FFN_NORM_ROWS, tm)
        for r in range(tm // chunk):
            rows = pl.ds(r * chunk, chunk)
            h = (_rms(acc_ref[buf, rows, :]) * nw_ref[...]).astype(BF)
            h_ref[rows, :] = h
            swiglu_into_acc(rows, h, *weights)

    @pl.when(j > 0)
    def _():
        swiglu_into_acc(slice(None), h_ref[...], *bf16_weights())

    @pl.when(j == n_j - 1)
    def _():
        if final:
            acc_ref[buf] = _rms(acc_ref[buf]) * fw_ref[...]
        out_copy(i, buf).start()

    @pl.when(step == n_i * n_j - 1)
    def _():
        out_copy(i, buf).wait()


def _ffn(x, nw, wg, wu, wd, layer, final_w=None):
    T, D = x.shape
    F = wg.shape[2]
    tm, tf = min(FFN_TM, T), FFN_TF
    n_i, n_j = T // tm, F // tf
    assert n_j >= 2, "the buffer hand-over happens at d_ff step 1"
    final = final_w is not None
    hbm = pl.BlockSpec(memory_space=pl.ANY)
    vec = pl.BlockSpec((1, D), lambda i, j: (0, 0))
    in_specs = [hbm, vec, hbm, hbm, hbm]
    args = [x, nw.reshape(1, D), wg, wu, wd]
    if final:
        in_specs.append(vec)
        args.append(final_w.reshape(1, D))
    return pl.pallas_call(
        functools.partial(_ffn_body, final=final, layer=layer, tm=tm, tf=tf, n_i=n_i, n_j=n_j),
        grid=(n_i, n_j),
        in_specs=in_specs,
        out_specs=hbm,
        out_shape=jax.ShapeDtypeStruct((T, D), F32),
        scratch_shapes=[
            pltpu.VMEM((2, tm, D), F32),
            pltpu.VMEM((tm, D), BF),
            pltpu.VMEM((2, D, tf), F32),
            pltpu.VMEM((2, D, tf), F32),
            pltpu.VMEM((2, tf, D), F32),
            pltpu.SemaphoreType.DMA((2, 3)),
            pltpu.SemaphoreType.DMA((2,)),
            pltpu.SemaphoreType.DMA((2,)),
        ],
        compiler_params=_params("arbitrary", "arbitrary"),
        name="ffn",
    )(*args)


def _inproj_body(x_ref, nw_ref, w_hbm, wa_ref, wal_ref, bal_ref, p_ref, la_ref, h_ref, w_buf, wsem,
                 *, tn, n_i, n_j):
    i = pl.program_id(0)
    j = pl.program_id(1)
    step = i * n_j + j
    nt = (((1,), (1,)), ((), ()))

    def w_copy(jj, slot):
        rows = pl.ds(pl.multiple_of(jj * tn, tn), tn)
        return pltpu.make_async_copy(w_hbm.at[rows, :], w_buf.at[slot], wsem.at[slot])

    @pl.when(step == 0)
    def _():
        w_copy(0, 0).start()
        w_copy(1, 1).start()

    @pl.when(step + 2 < n_i * n_j)
    def _():
        j_ahead = jnp.where(j + 2 >= n_j, j + 2 - n_j, j + 2)
        w_copy(j_ahead, (step + 2) % 3).start()

    slot = step % 3
    w_copy(j, slot).wait()

    @pl.when(j == 0)
    def _():
        tm = x_ref.shape[0]
        chunk = min(PROJ_NORM_ROWS, tm)
        w = w_buf[slot]
        for r in range(tm // chunk):
            rows = pl.ds(r * chunk, chunk)
            h = (_rms(x_ref[rows, :]) * nw_ref[...]).astype(BF)
            h_ref[rows, :] = h
            p_ref[rows, :] = lax.dot_general(h, w, nt, preferred_element_type=F32).astype(BF)
            a_low = lax.dot_general(h, wa_ref[...], nt, preferred_element_type=F32)
            z = jnp.dot(a_low.astype(BF), wal_ref[...], preferred_element_type=F32) + bal_ref[...]
            la_ref[rows, :] = (jnp.minimum(z, 0.0) - jnp.log1p(jnp.exp(-jnp.abs(z)))) / GLA_GATE_NORM

    @pl.when(j > 0)
    def _():
        p_ref[...] = lax.dot_general(h_ref[...], w_buf[slot], nt, preferred_element_type=F32).astype(BF)


def _inproj(x, nw, w, wa, wal, bal):
    T, D = x.shape
    tm, tn = min(PROJ_TM, T), PROJ_TN
    n_i, n_j = T // tm, N_PROJ // tn
    assert n_j >= 2, "the ring is primed with the first two weight tiles"
    return pl.pallas_call(
        functools.partial(_inproj_body, tn=tn, n_i=n_i, n_j=n_j),
        grid=(n_i, n_j),
        in_specs=[
            pl.BlockSpec((tm, D), lambda i, j: (i, 0)),
            pl.BlockSpec((1, D), lambda i, j: (0, 0)),
            pl.BlockSpec(memory_space=pl.ANY),
            pl.BlockSpec((LANES, D), lambda i, j: (0, 0)),
            pl.BlockSpec((LANES, GLA_K_DIM), lambda i, j: (0, 0)),
            pl.BlockSpec((1, GLA_K_DIM), lambda i, j: (0, 0)),
        ],
        out_specs=[
            pl.BlockSpec((tm, tn), lambda i, j: (i, j)),
            pl.BlockSpec((tm, GLA_K_DIM), lambda i, j: (i, 0)),
        ],
        out_shape=[
            jax.ShapeDtypeStruct((T, N_PROJ), BF),
            jax.ShapeDtypeStruct((T, GLA_K_DIM), F32),
        ],
        scratch_shapes=[pltpu.VMEM((tm, D), BF), pltpu.VMEM((3, tn, D), BF),
                        pltpu.SemaphoreType.DMA((3,))],
        compiler_params=_params("arbitrary", "arbitrary"),
        name="inproj",
    )(x, nw.reshape(1, D), w, wa, wal, bal.reshape(1, GLA_K_DIM))


def _conv_body(a_ref, g_ref, w_ref, b_ref, lnw_ref, lnb_ref, o_ref, zs_ref, wb_ref, y_ref, *, ts):
    s = pl.program_id(1)
    halo, rows = CONV_HALO, CONV_ROWS

    @pl.when(s == 0)
    def _():
        zs_ref[:, 0:halo, :] = jnp.zeros((SUBLANES, halo, CONV_CH), F32)
        for j in range(CONV_WIDTH):
            wb_ref[j] = jnp.broadcast_to(w_ref[j:j + 1, :], (SUBLANES, CONV_CH))

    @pl.when(s > 0)
    def _():
        zs_ref[:, 0:halo, :] = zs_ref[:, ts:ts + halo, :]

    a = a_ref[...].astype(F32)
    g = g_ref[...].astype(F32)
    zs_ref[0, halo:halo + ts, :] = a * jax.nn.sigmoid(g)

    def chunk(c, carry):
        r0 = pl.multiple_of(c * rows, rows)
        win = zs_ref[0, pl.ds(halo - SUBLANES + r0, rows + SUBLANES), :]
        for p in range(1, SUBLANES):
            zs_ref[p, pl.ds(halo + r0, rows), :] = win[SUBLANES - p:SUBLANES - p + rows, :]
        accs = [jnp.zeros((SUBLANES, CONV_CH), F32) + b_ref[...] for _ in range(rows // SUBLANES)]
        for d in range(CONV_WIDTH):
            wj = wb_ref[CONV_WIDTH - 1 - d]
            base = halo + r0 - SUBLANES * (d // SUBLANES)
            for i in range(rows // SUBLANES):
                zd = zs_ref[d % SUBLANES, pl.ds(base + i * SUBLANES, SUBLANES), :]
                accs[i] = accs[i] + wj * zd
        for i in range(rows // SUBLANES):
            y_ref[pl.ds(r0 + i * SUBLANES, SUBLANES), :] = accs[i]
        return carry

    lax.fori_loop(0, ts // rows, chunk, 0)

    y = y_ref[...]
    mu = jnp.mean(y, axis=-1, keepdims=True)
    dlt = y - mu
    var = jnp.mean(dlt * dlt, axis=-1, keepdims=True)
    zn = dlt * lax.rsqrt(var + EPS) * lnw_ref[...] + lnb_ref[...]
    o_ref[...] = (zn * jax.nn.sigmoid(zn)).astype(BF)


def _conv(proj, B, S, dw_w, dw_b, ln_w, ln_b):
    T = B * S
    ts = min(CONV_TS, S)
    nS = S // ts
    ca, cg = COL_CONV // CONV_CH, COL_CONV // CONV_CH + 1
    vec = pl.BlockSpec((1, CONV_CH), lambda b, s: (0, 0))
    return pl.pallas_call(
        functools.partial(_conv_body, ts=ts),
        grid=(B, nS),
        in_specs=[
            pl.BlockSpec((ts, CONV_CH), lambda b, s: (b * nS + s, ca)),
            pl.BlockSpec((ts, CONV_CH), lambda b, s: (b * nS + s, cg)),
            pl.BlockSpec((CONV_WIDTH, CONV_CH), lambda b, s: (0, 0)),
            vec, vec, vec,
        ],
        out_specs=pl.BlockSpec((ts, CONV_CH), lambda b, s: (b * nS + s, 0)),
        out_shape=jax.ShapeDtypeStruct((T, CONV_CH), BF),
        scratch_shapes=[pltpu.VMEM((SUBLANES, ts + CONV_HALO, CONV_CH), F32),
                        pltpu.VMEM((CONV_WIDTH, SUBLANES, CONV_CH), F32),
                        pltpu.VMEM((ts, CONV_CH), F32)],
        compiler_params=_params("parallel", "arbitrary"),
        name="conv",
    )(proj, proj, dw_w.reshape(CONV_WIDTH, CONV_CH), dw_b.reshape(1, CONV_CH),
      ln_w.reshape(1, CONV_CH), ln_b.reshape(1, CONV_CH))


def _attn_tile(first, sink_ref, q_ref, kp_ref, kc_ref, vp_ref, vc_ref, o_ref, nblk):
    blk, dh = ATT_BLOCK, ATT_HEAD_DIM
    group = ATT_HEADS // ATT_KV_HEADS
    qi = lax.broadcasted_iota(jnp.int32, (blk, blk), 0)
    kj = lax.broadcasted_iota(jnp.int32, (blk, blk), 1)
    cur = kj <= qi
    dist_in = ((qi - kj) & (blk - 1)).astype(F32)
    dist_first = jnp.where(jnp.logical_and(first, kj > qi), jnp.inf, dist_in)
    cur_b = jnp.where(cur, 1.0, 0.0).astype(BF)
    prev_b = jnp.where(cur, 0.0, 1.0).astype(BF)
    lane = lax.broadcasted_iota(jnp.int32, (2 * blk, 2 * dh), 1)
    half_b = (jnp.where(lane < dh, 1.0, 0.0).astype(BF), jnp.where(lane < dh, 0.0, 1.0).astype(BF))
    qscale = jnp.asarray(dh ** -0.5, BF)
    nt = (((1,), (1,)), ((), ()))
    for n in range(nblk):
        rows = slice(n * blk, (n + 1) * blk)
        before = slice((n - 1) * blk, n * blk)
        dist = dist_first if n == 0 else dist_in
        for p in range(ATT_KV_HEADS // 2):
            col = slice(p * 2 * dh, (p + 1) * 2 * dh)
            k_prev = kp_ref[:, col] if n == 0 else kc_ref[before, col]
            v_prev = vp_ref[:, col] if n == 0 else vc_ref[before, col]
            kcat = jnp.concatenate([k_prev, kc_ref[rows, col]], axis=0)
            vcat = jnp.concatenate([v_prev, vc_ref[rows, col]], axis=0)
            k_side = [kcat * hb for hb in half_b]
            v_side = [vcat * hb for hb in half_b]
            for g in range(group):
                c = p * group + g
                qcol = slice(c * 2 * dh, (c + 1) * 2 * dh)
                q2 = q_ref[rows, qcol] * qscale
                acc = None
                for side in range(2):
                    h = (2 * p + side) * group + g
                    slope = 2.0 ** (-8.0 * (h + 1) / ATT_HEADS)
                    s = lax.dot_general(q2, k_side[side], nt, preferred_element_type=F32)
                    t = jnp.where(cur, s[:, blk:], s[:, :blk]) - slope * dist
                    sink = sink_ref[h]
                    m = jnp.maximum(jnp.max(t, axis=-1, keepdims=True), sink)
                    pe = jnp.exp(t - m)
                    den = jnp.sum(pe, axis=-1, keepdims=True) + jnp.exp(sink - m)
                    pb = pe.astype(BF)
                    pcat = jnp.concatenate([pb * prev_b, pb * cur_b], axis=1)
                    o = jnp.dot(pcat, v_side[side], preferred_element_type=F32) / den
                    acc = o if acc is None else acc + o
                o_ref[rows, qcol] = acc.astype(BF)


def _gla_block(q_ref, k_ref, v_ref, r_ref, la_ref, nw_ref, o_ref, st_ref, nchunk):
    ck, dk, dv = GLA_CHUNK, GLA_DK, GLA_DV
    lb = nchunk * ck
    ri = lax.broadcasted_iota(jnp.int32, (lb, lb), 0)
    ci = lax.broadcasted_iota(jnp.int32, (lb, lb), 1)
    shift = ck.bit_length() - 1
    causal = jnp.logical_and(ci <= ri, jnp.right_shift(ri, shift) == jnp.right_shift(ci, shift))
    tril = jnp.where(causal, 1.0, 0.0).astype(BF)
    nt = (((1,), (1,)), ((), ()))
    tn = (((0,), (0,)), ((), ()))
    qscale = dk ** -0.5

    la = la_ref[...]
    hi = la.astype(BF)
    r1 = la - hi.astype(F32)
    mid = r1.astype(BF)
    lo = (r1 - mid.astype(F32)).astype(BF)
    b = (jnp.dot(tril, hi, preferred_element_type=F32)
         + jnp.dot(tril, mid, preferred_element_type=F32)
         + jnp.dot(tril, lo, preferred_element_type=F32))
    chunks = [slice(c * ck, (c + 1) * ck) for c in range(nchunk)]
    b_last = [b[(c + 1) * ck - 1:(c + 1) * ck, :] for c in range(nchunk)]
    eb = jnp.exp(b)
    enb = jnp.exp(-b)
    ee = jnp.concatenate([jnp.exp(b_last[c] - b[chunks[c], :]) for c in range(nchunk)], axis=0)
    dec = [jnp.exp(b_last[c]) for c in range(nchunk)]
    for h in range(GLA_HEADS):
        sk = slice(h * dk, (h + 1) * dk)
        sv = slice(h * dv, (h + 1) * dv)
        q = q_ref[:, sk].astype(F32) * qscale
        k = k_ref[:, sk].astype(F32)
        v = v_ref[:, sv]
        qt = (q * eb[:, sk]).astype(BF)
        kt = (k * enb[:, sk]).astype(BF)
        ke = (k * ee[:, sk]).astype(BF)
        a = lax.dot_general(qt, kt, nt, preferred_element_type=F32)
        a = jnp.where(causal, a, 0.0).astype(BF)
        o = jnp.dot(a, v, preferred_element_type=F32)
        st = st_ref[h]
        inter = []
        for c in range(nchunk):
            inter.append(lax.dot_general(qt[chunks[c], :], st.astype(BF), nt, preferred_element_type=F32))
            st = st * dec[c][:, sk] + lax.dot_general(v[chunks[c], :], ke[chunks[c], :], tn,
                                                      preferred_element_type=F32)
        st_ref[h] = st
        o = o + jnp.concatenate(inter, axis=0)
        on = o * lax.rsqrt(jnp.mean(o * o, axis=-1, keepdims=True) + EPS)
        r = r_ref[:, sv].astype(F32)
        o_ref[:, sv] = (on * nw_ref[:, sv] * (r * jax.nn.sigmoid(r))).astype(BF)


def _mix_body(sink_ref, x_ref, ga_ref, gb_ref, gc_ref, ba_ref, bb_ref, bc_ref, uc_ref,
              q_ref, kp_ref, kc_ref, vp_ref, vc_ref, gq_ref, gk_ref, gv_ref, gr_ref, la_ref, nw_ref,
              wpw_ref, wao_ref, wgo_ref, wout_ref, o_ref, ua_ref, ug_ref, st_ref, *, tiles_per_seq):
    first = pl.program_id(0) % tiles_per_seq == 0

    @pl.when(first)
    def _():
        st_ref[...] = jnp.zeros_like(st_ref)

    tm = x_ref.shape[0]
    _attn_tile(first, sink_ref, q_ref, kp_ref, kc_ref, vp_ref, vc_ref, ua_ref, tm // ATT_BLOCK)
    _gla_block(gq_ref, gk_ref, gv_ref, gr_ref, la_ref, nw_ref, ug_ref, st_ref, tm // GLA_CHUNK)

    def gate(p_ref, b_ref):
        return jax.nn.sigmoid(p_ref[...].astype(F32) + b_ref[...])

    m = gate(ga_ref, ba_ref) * jnp.dot(uc_ref[...], wpw_ref[...], preferred_element_type=F32)
    m = m + gate(gb_ref, bb_ref) * jnp.dot(ua_ref[...], wao_ref[...], preferred_element_type=F32)
    m = m + gate(gc_ref, bc_ref) * jnp.dot(ug_ref[...], wgo_ref[...], preferred_element_type=F32)
    o_ref[...] = x_ref[...] + jnp.dot(m.astype(BF), wout_ref[...], preferred_element_type=F32)


def _mix(x, proj, la, B, S, b_gate, uc, sinks, gla_norm_w, w_pw, w_ao, w_go, w_out):
    T, D = x.shape
    tm = MIX_TM
    tiles_per_seq = S // tm
    per_tile = tm // ATT_BLOCK
    const = lambda shape: pl.BlockSpec(shape, lambda i: (0, 0), pipeline_mode=pl.Buffered(1))
    act = lambda w, off: pl.BlockSpec((tm, w), lambda i: (i, off // w))
    before = lambda off: pl.BlockSpec(
        (ATT_BLOCK, ATT_KV_DIM), lambda i: (jnp.maximum(i * per_tile - 1, 0), off // ATT_KV_DIM))
    bias = lambda c: pl.BlockSpec((1, D), lambda i: (0, c))
    bg = b_gate.reshape(1, 3 * D)
    return pl.pallas_call(
        functools.partial(_mix_body, tiles_per_seq=tiles_per_seq),
        grid=(T // tm,),
        in_specs=[
            pl.BlockSpec(memory_space=pltpu.SMEM),
            act(D, 0),
            act(D, COL_GATE), act(D, COL_GATE + D), act(D, COL_GATE + 2 * D),
            bias(0), bias(1), bias(2),
            act(CONV_CH, 0),
            act(ATT_Q_DIM, COL_Q),
            before(COL_K), act(ATT_KV_DIM, COL_K), before(COL_V), act(ATT_KV_DIM, COL_V),
            act(GLA_K_DIM, COL_GQ), act(GLA_K_DIM, COL_GK), act(GLA_V_DIM, COL_GV), act(GLA_V_DIM, COL_GR),
            act(GLA_K_DIM, 0),
            pl.BlockSpec((1, GLA_V_DIM), lambda i: (0, 0)),
            const((CONV_CH, D)), const((ATT_Q_DIM, D)), const((GLA_V_DIM, D)), const((D, D)),
        ],
        out_specs=act(D, 0),
        out_shape=jax.ShapeDtypeStruct((T, D), F32),
        scratch_shapes=[pltpu.VMEM((tm, ATT_Q_DIM), BF), pltpu.VMEM((tm, GLA_V_DIM), BF),
                        pltpu.VMEM((GLA_HEADS, GLA_DV, GLA_DK), F32)],
        compiler_params=_params("arbitrary"),
        name="mix",
    )(sinks, x, proj, proj, proj, bg, bg, bg, uc, proj, proj, proj, proj, proj,
      proj, proj, proj, proj, la, gla_norm_w.reshape(1, GLA_V_DIM), w_pw, w_ao, w_go, w_out)


def _pair_heads(w, axis):
    group = ATT_HEADS // ATT_KV_HEADS
    split = (ATT_KV_HEADS // 2, 2, group, ATT_HEAD_DIM)
    if axis == 0:
        return w.reshape(*split, w.shape[1]).transpose(0, 2, 1, 3, 4).reshape(w.shape)
    return w.reshape(w.shape[0], *split).transpose(0, 1, 3, 2, 4).reshape(w.shape)


def _regroup_body(w_ref, main_ref, wa_ref):
    c = 2 * CONV_CH
    o_q = c
    o_k = o_q + ATT_Q_DIM
    o_v = o_k + ATT_KV_DIM
    o_gq = o_v + ATT_KV_DIM
    o_gk = o_gq + GLA_K_DIM
    o_gv = o_gk + GLA_K_DIM
    o_gr = o_gv + GLA_V_DIM
    o_ga = o_gr + GLA_V_DIM
    o_gate = o_ga + GLA_RANK

    def put(dst, src, n):
        main_ref[dst:dst + n, :] = w_ref[src:src + n, :].astype(BF)

    put(COL_GATE, o_gate, 3 * D_MODEL)
    put(COL_CONV, 0, c)
    group, dh = ATT_HEADS // ATT_KV_HEADS, ATT_HEAD_DIM
    for col in range(ATT_HEADS // 2):
        p, g = divmod(col, group)
        for side in range(2):
            h = (2 * p + side) * group + g
            put(COL_Q + (2 * col + side) * dh, o_q + h * dh, dh)
    put(COL_GV, o_gv, GLA_V_DIM)
    put(COL_GR, o_gr, GLA_V_DIM)
    put(COL_GQ, o_gq, GLA_K_DIM)
    put(COL_GK, o_gk, GLA_K_DIM)
    put(COL_K, o_k, ATT_KV_DIM)
    put(COL_V, o_v, ATT_KV_DIM)
    cols = w_ref.shape[1]
    wa_ref[...] = jnp.concatenate(
        [w_ref[o_ga:o_gate, :], jnp.zeros((LANES - GLA_RANK, cols), F32)], axis=0).astype(BF)


def _regroup_w_in(w_in, layer):
    w_t = jnp.swapaxes(w_in, 1, 2)
    _, d_in, D = w_t.shape
    tc = REGROUP_COLS
    return pl.pallas_call(
        _regroup_body,
        grid=(D // tc,),
        in_specs=[pl.BlockSpec((None, d_in, tc), lambda i: (layer, 0, i))],
        out_specs=[pl.BlockSpec((N_PROJ, tc), lambda i: (0, i)),
                   pl.BlockSpec((LANES, tc), lambda i: (0, i))],
        out_shape=[jax.ShapeDtypeStruct((N_PROJ, D), BF), jax.ShapeDtypeStruct((LANES, D), BF)],
        compiler_params=_params("parallel"),
        name="regroup",
    )(w_t)


def kernel(x, ffn1_norm, ffn1_w_gate, ffn1_w_up, ffn1_w_down, mix_norm, w_in, b_gate, conv_dw_w, conv_dw_b, conv_ln_w, conv_ln_b, conv_w_pw, att_sinks, att_w_o, gla_w_alpha, gla_b_alpha, gla_norm_w, gla_w_o, w_out, ffn2_norm, ffn2_w_gate, ffn2_w_up, ffn2_w_down, final_norm):
    B, S, D = x.shape
    xt = x.reshape(B * S, D)
    for l in range(DEPTH):
        xt = _ffn(xt, ffn1_norm[l], ffn1_w_gate, ffn1_w_up, ffn1_w_down, l)
        w_main, w_a = _regroup_w_in(w_in, l)
        w_alpha = jnp.pad(gla_w_alpha[l], ((0, LANES - GLA_RANK), (0, 0))).astype(BF)
        proj, la = _inproj(xt, mix_norm[l], w_main, w_a, w_alpha, gla_b_alpha[l])
        uc = _conv(proj, B, S, conv_dw_w[l], conv_dw_b[l], conv_ln_w[l], conv_ln_b[l])
        xt = _mix(xt, proj, la, B, S, b_gate[l], uc, att_sinks[l], gla_norm_w[l], conv_w_pw[l].astype(BF),
                    _pair_heads(att_w_o[l], axis=0).astype(BF),
                    gla_w_o[l].astype(BF), w_out[l].astype(BF))
        xt = _ffn(xt, ffn2_norm[l], ffn2_w_gate, ffn2_w_up, ffn2_w_down, l,
                  final_w=final_norm if l == DEPTH - 1 else None)
    return xt.reshape(B, S, D)
```

```python
import functools

import jax
import jax.numpy as jnp
from jax import lax
from jax.experimental import pallas as pl
from jax.experimental.pallas import tpu as pltpu

BF = jnp.bfloat16
F32 = jnp.float32

D_MODEL = 2048
DEPTH = 2
CONV_CH = 1024
CONV_WIDTH = 31
ATT_HEADS = 16
ATT_KV_HEADS = 4
ATT_HEAD_DIM = 64
ATT_BLOCK = 128
ATT_Q_DIM = 1024
ATT_KV_DIM = 256
GLA_HEADS = 4
GLA_DK = 128
GLA_DV = 256
GLA_K_DIM = 512
GLA_V_DIM = 1024
GLA_RANK = 16
GLA_GATE_NORM = 16.0
GLA_CHUNK = 64
D_FF = 5632
EPS = 1e-6

VMEM_LIMIT_BYTES = 60 * 1024 * 1024
LANES = 128
SUBLANES = 8

N_PROJ = 12800
COL_GATE = 0
COL_CONV = 6144
COL_Q = 8192
COL_GV = 9216
COL_GR = 10240
COL_GQ = 11264
COL_GK = 11776
COL_K = 12288
COL_V = 12544

FFN_TM = 1024
FFN_TF = 512
FFN_NORM_ROWS = 256
PROJ_TM = 1024
PROJ_TN = 1280
CONV_TS = 512
CONV_HALO = 32
CONV_ROWS = 32
MIX_TM = 256
REGROUP_COLS = 256


def _params(*sem):
    return pltpu.CompilerParams(dimension_semantics=sem, vmem_limit_bytes=VMEM_LIMIT_BYTES)


def _rms(x):
    return x * lax.rsqrt(jnp.mean(x * x, axis=-1, keepdims=True) + EPS)


def _ffn_body(x_hbm, nw_ref, wg_hbm, wu_hbm, wd_hbm, *rest, final, layer, tm, tf, n_i, n_j):
    if final:
        fw_ref, o_hbm, acc_ref, h_ref, wg_buf, wu_buf, wd_buf, wsem, xsem, osem = rest
    else:
        o_hbm, acc_ref, h_ref, wg_buf, wu_buf, wd_buf, wsem, xsem, osem = rest
    i = pl.program_id(0)
    j = pl.program_id(1)
    step = i * n_j + j
    buf = i % 2
    slot = step % 2

    def weight_copies(jj, sl):
        cols = pl.ds(pl.multiple_of(jj * tf, tf), tf)
        return (pltpu.make_async_copy(wg_hbm.at[layer, :, cols], wg_buf.at[sl], wsem.at[sl, 0]),
                pltpu.make_async_copy(wu_hbm.at[layer, :, cols], wu_buf.at[sl], wsem.at[sl, 1]),
                pltpu.make_async_copy(wd_hbm.at[layer, cols, :], wd_buf.at[sl], wsem.at[sl, 2]))

    def tile_rows(ii):
        return pl.ds(pl.multiple_of(ii * tm, tm), tm)

    def x_copy(ii, b):
        return pltpu.make_async_copy(x_hbm.at[tile_rows(ii), :], acc_ref.at[b], xsem.at[b])

    def out_copy(ii, b):
        return pltpu.make_async_copy(acc_ref.at[b], o_hbm.at[tile_rows(ii), :], osem.at[b])

    @pl.when(step == 0)
    def _():
        x_copy(0, 0).start()
        for cp in weight_copies(0, 0):
            cp.start()

    @pl.when(step + 1 < n_i * n_j)
    def _():
        j_next = jnp.where(j + 1 == n_j, 0, j + 1)
        for cp in weight_copies(j_next, 1 - slot):
            cp.start()

    @pl.when(j == 1)
    def _():
        @pl.when(i >= 1)
        def _():
            out_copy(i - 1, 1 - buf).wait()

        @pl.when(i + 1 < n_i)
        def _():
            x_copy(i + 1, 1 - buf).start()

    for cp in weight_copies(j, slot):
        cp.wait()

    def swiglu_into_acc(rows, h, wg, wu, wd):
        g = jnp.dot(h, wg, preferred_element_type=F32)
        u = jnp.dot(h, wu, preferred_element_type=F32)
        a = (0.5 * (g * jax.nn.sigmoid(g) * u)).astype(BF)
        acc_ref[buf, rows, :] += jnp.dot(a, wd, preferred_element_type=F32)

    def bf16_weights():
        return wg_buf[slot].astype(BF), wu_buf[slot].astype(BF), wd_buf[slot].astype(BF)

    @pl.when(j == 0)
    def _():
        x_copy(i, buf).wait()
        weights = bf16_weights()
        chunk = min(FFN_NORM_ROWS, tm)
        for r in range(tm // chunk):
            rows = pl.ds(r * chunk, chunk)
            h = (_rms(acc_ref[buf, rows, :]) * nw_ref[...]).astype(BF)
            h_ref[rows, :] = h
            swiglu_into_acc(rows, h, *weights)

    @pl.when(j > 0)
    def _():
        swiglu_into_acc(slice(None), h_ref[...], *bf16_weights())

    @pl.when(j == n_j - 1)
    def _():
        if final:
            acc_ref[buf] = _rms(acc_ref[buf]) * fw_ref[...]
        out_copy(i, buf).start()

    @pl.when(step == n_i * n_j - 1)
    def _():
        out_copy(i, buf).wait()


def _ffn(x, nw, wg, wu, wd, layer, final_w=None):
    T, D = x.shape
    F = wg.shape[2]
    tm, tf = min(FFN_TM, T), FFN_TF
    n_i, n_j = T // tm, F // tf
    assert n_j >= 2, "the buffer hand-over happens at d_ff step 1"
    final = final_w is not None
    hbm = pl.BlockSpec(memory_space=pl.ANY)
    vec = pl.BlockSpec((1, D), lambda i, j: (0, 0))
    in_specs = [hbm, vec, hbm, hbm, hbm]
    args = [x, nw.reshape(1, D), wg, wu, wd]
    if final:
        in_specs.append(vec)
        args.append(final_w.reshape(1, D))
    return pl.pallas_call(
        functools.partial(_ffn_body, final=final, layer=layer, tm=tm, tf=tf, n_i=n_i, n_j=n_j),
        grid=(n_i, n_j),
        in_specs=in_specs,
        out_specs=hbm,
        out_shape=jax.ShapeDtypeStruct((T, D), F32),
        scratch_shapes=[
            pltpu.VMEM((2, tm, D), F32),
            pltpu.VMEM((tm, D), BF),
            pltpu.VMEM((2, D, tf), F32),
            pltpu.VMEM((2, D, tf), F32),
            pltpu.VMEM((2, tf, D), F32),
            pltpu.SemaphoreType.DMA((2, 3)),
            pltpu.SemaphoreType.DMA((2,)),
            pltpu.SemaphoreType.DMA((2,)),
        ],
        compiler_params=_params("arbitrary", "arbitrary"),
        name="ffn",
    )(*args)


def _inproj_body(x_ref, nw_ref, w_hbm, wa_ref, wal_ref, bal_ref, p_ref, la_ref, h_ref, w_buf, wsem,
                 *, tn, n_i, n_j):
    i = pl.program_id(0)
    j = pl.program_id(1)
    step = i * n_j + j
    nt = (((1,), (1,)), ((), ()))

    def w_copy(jj, slot):
        rows = pl.ds(pl.multiple_of(jj * tn, tn), tn)
        return pltpu.make_async_copy(w_hbm.at[rows, :], w_buf.at[slot], wsem.at[slot])

    @pl.when(step == 0)
    def _():
        w_copy(0, 0).start()
        w_copy(1, 1).start()

    @pl.when(step + 2 < n_i * n_j)
    def _():
        j_ahead = jnp.where(j + 2 >= n_j, j + 2 - n_j, j + 2)
        w_copy(j_ahead, (step + 2) % 3).start()

    @pl.when(j == 0)
    def _():
        h = (_rms(x_ref[...]) * nw_ref[...]).astype(BF)
        h_ref[...] = h
        a_low = lax.dot_general(h, wa_ref[...], nt, preferred_element_type=F32)
        z = jnp.dot(a_low.astype(BF), wal_ref[...], preferred_element_type=F32) + bal_ref[...]
        la_ref[...] = (jnp.minimum(z, 0.0) - jnp.log1p(jnp.exp(-jnp.abs(z)))) / GLA_GATE_NORM

    slot = step % 3
    w_copy(j, slot).wait()
    p_ref[...] = lax.dot_general(h_ref[...], w_buf[slot], nt, preferred_element_type=F32).astype(BF)


def _inproj(x, nw, w, wa, wal, bal):
    T, D = x.shape
    tm, tn = min(PROJ_TM, T), PROJ_TN
    n_i, n_j = T // tm, N_PROJ // tn
    assert n_j >= 2, "the ring is primed with the first two weight tiles"
    return pl.pallas_call(
        functools.partial(_inproj_body, tn=tn, n_i=n_i, n_j=n_j),
        grid=(n_i, n_j),
        in_specs=[
            pl.BlockSpec((tm, D), lambda i, j: (i, 0)),
            pl.BlockSpec((1, D), lambda i, j: (0, 0)),
            pl.BlockSpec(memory_space=pl.ANY),
            pl.BlockSpec((LANES, D), lambda i, j: (0, 0)),
            pl.BlockSpec((LANES, GLA_K_DIM), lambda i, j: (0, 0)),
            pl.BlockSpec((1, GLA_K_DIM), lambda i, j: (0, 0)),
        ],
        out_specs=[
            pl.BlockSpec((tm, tn), lambda i, j: (i, j)),
            pl.BlockSpec((tm, GLA_K_DIM), lambda i, j: (i, 0)),
        ],
        out_shape=[
            jax.ShapeDtypeStruct((T, N_PROJ), BF),
            jax.ShapeDtypeStruct((T, GLA_K_DIM), F32),
        ],
        scratch_shapes=[pltpu.VMEM((tm, D), BF), pltpu.VMEM((3, tn, D), BF),
                        pltpu.SemaphoreType.DMA((3,))],
        compiler_params=_params("arbitrary", "arbitrary"),
        name="inproj",
    )(x, nw.reshape(1, D), w, wa, wal, bal.reshape(1, GLA_K_DIM))


def _conv_body(a_ref, g_ref, w_ref, b_ref, lnw_ref, lnb_ref, o_ref, zs_ref, wb_ref, y_ref, *, ts):
    s = pl.program_id(1)
    halo, rows = CONV_HALO, CONV_ROWS

    @pl.when(s == 0)
    def _():
        zs_ref[:, 0:halo, :] = jnp.zeros((SUBLANES, halo, CONV_CH), F32)
        for j in range(CONV_WIDTH):
            wb_ref[j] = jnp.broadcast_to(w_ref[j:j + 1, :], (SUBLANES, CONV_CH))

    @pl.when(s > 0)
    def _():
        zs_ref[:, 0:halo, :] = zs_ref[:, ts:ts + halo, :]

    a = a_ref[...].astype(F32)
    g = g_ref[...].astype(F32)
    zs_ref[0, halo:halo + ts, :] = a * jax.nn.sigmoid(g)

    def chunk(c, carry):
        r0 = pl.multiple_of(c * rows, rows)
        win = zs_ref[0, pl.ds(halo - SUBLANES + r0, rows + SUBLANES), :]
        for p in range(1, SUBLANES):
            zs_ref[p, pl.ds(halo + r0, rows), :] = win[SUBLANES - p:SUBLANES - p + rows, :]
        accs = [jnp.zeros((SUBLANES, CONV_CH), F32) + b_ref[...] for _ in range(rows // SUBLANES)]
        for d in range(CONV_WIDTH):
            wj = wb_ref[CONV_WIDTH - 1 - d]
            base = halo + r0 - SUBLANES * (d // SUBLANES)
            for i in range(rows // SUBLANES):
                zd = zs_ref[d % SUBLANES, pl.ds(base + i * SUBLANES, SUBLANES), :]
                accs[i] = accs[i] + wj * zd
        for i in range(rows // SUBLANES):
            y_ref[pl.ds(r0 + i * SUBLANES, SUBLANES), :] = accs[i]
        return carry

    lax.fori_loop(0, ts // rows, chunk, 0)

    y = y_ref[...]
    mu = jnp.mean(y, axis=-1, keepdims=True)
    dlt = y - mu
    var = jnp.mean(dlt * dlt, axis=-1, keepdims=True)
    zn = dlt * lax.rsqrt(var + EPS) * lnw_ref[...] + lnb_ref[...]
    o_ref[...] = (zn * jax.nn.sigmoid(zn)).astype(BF)


def _conv(proj, B, S, dw_w, dw_b, ln_w, ln_b):
    T = B * S
    ts = min(CONV_TS, S)
    nS = S // ts
    ca, cg = COL_CONV // CONV_CH, COL_CONV // CONV_CH + 1
    vec = pl.BlockSpec((1, CONV_CH), lambda b, s: (0, 0))
    return pl.pallas_call(
        functools.partial(_conv_body, ts=ts),
        grid=(B, nS),
        in_specs=[
            pl.BlockSpec((ts, CONV_CH), lambda b, s: (b * nS + s, ca)),
            pl.BlockSpec((ts, CONV_CH), lambda b, s: (b * nS + s, cg)),
            pl.BlockSpec((CONV_WIDTH, CONV_CH), lambda b, s: (0, 0)),
            vec, vec, vec,
        ],
        out_specs=pl.BlockSpec((ts, CONV_CH), lambda b, s: (b * nS + s, 0)),
        out_shape=jax.ShapeDtypeStruct((T, CONV_CH), BF),
        scratch_shapes=[pltpu.VMEM((SUBLANES, ts + CONV_HALO, CONV_CH), F32),
                        pltpu.VMEM((CONV_WIDTH, SUBLANES, CONV_CH), F32),
                        pltpu.VMEM((ts, CONV_CH), F32)],
        compiler_params=_params("parallel", "arbitrary"),
        name="conv",
    )(proj, proj, dw_w.reshape(CONV_WIDTH, CONV_CH), dw_b.reshape(1, CONV_CH),
      ln_w.reshape(1, CONV_CH), ln_b.reshape(1, CONV_CH))


def _attn_tile(first, sink_ref, q_ref, kp_ref, kc_ref, vp_ref, vc_ref, o_ref, nblk):
    blk, dh = ATT_BLOCK, ATT_HEAD_DIM
    group = ATT_HEADS // ATT_KV_HEADS
    qi = lax.broadcasted_iota(jnp.int32, (blk, blk), 0)
    kj = lax.broadcasted_iota(jnp.int32, (blk, blk), 1)
    cur = kj <= qi
    dist_in = ((qi - kj) & (blk - 1)).astype(F32)
    dist_first = jnp.where(jnp.logical_and(first, kj > qi), jnp.inf, dist_in)
    cur_b = jnp.where(cur, 1.0, 0.0).astype(BF)
    prev_b = jnp.where(cur, 0.0, 1.0).astype(BF)
    lane = lax.broadcasted_iota(jnp.int32, (2 * blk, 2 * dh), 1)
    half_b = (jnp.where(lane < dh, 1.0, 0.0).astype(BF), jnp.where(lane < dh, 0.0, 1.0).astype(BF))
    qscale = jnp.asarray(dh ** -0.5, BF)
    nt = (((1,), (1,)), ((), ()))
    for n in range(nblk):
        rows = slice(n * blk, (n + 1) * blk)
        before = slice((n - 1) * blk, n * blk)
        dist = dist_first if n == 0 else dist_in
        for p in range(ATT_KV_HEADS // 2):
            col = slice(p * 2 * dh, (p + 1) * 2 * dh)
            k_prev = kp_ref[:, col] if n == 0 else kc_ref[before, col]
            v_prev = vp_ref[:, col] if n == 0 else vc_ref[before, col]
            kcat = jnp.concatenate([k_prev, kc_ref[rows, col]], axis=0)
            vcat = jnp.concatenate([v_prev, vc_ref[rows, col]], axis=0)
            k_side = [kcat * hb for hb in half_b]
            v_side = [vcat * hb for hb in half_b]
            for g in range(group):
                c = p * group + g
                qcol = slice(c * 2 * dh, (c + 1) * 2 * dh)
                q2 = q_ref[rows, qcol] * qscale
                acc = None
                for side in range(2):
                    h = (2 * p + side) * group + g
                    slope = 2.0 ** (-8.0 * (h + 1) / ATT_HEADS)
                    s = lax.dot_general(q2, k_side[side], nt, preferred_element_type=F32)
                    t = jnp.where(cur, s[:, blk:], s[:, :blk]) - slope * dist
                    sink = sink_ref[h]
                    m = jnp.maximum(jnp.max(t, axis=-1, keepdims=True), sink)
                    pe = jnp.exp(t - m)
                    den = jnp.sum(pe, axis=-1, keepdims=True) + jnp.exp(sink - m)
                    pb = pe.astype(BF)
                    pcat = jnp.concatenate([pb * prev_b, pb * cur_b], axis=1)
                    o = jnp.dot(pcat, v_side[side], preferred_element_type=F32) / den
                    acc = o if acc is None else acc + o
                o_ref[rows, qcol] = acc.astype(BF)


def _gla_block(q_ref, k_ref, v_ref, r_ref, la_ref, nw_ref, o_ref, st_ref, nchunk):
    ck, dk, dv = GLA_CHUNK, GLA_DK, GLA_DV
    lb = nchunk * ck
    ri = lax.broadcasted_iota(jnp.int32, (lb, lb), 0)
    ci = lax.broadcasted_iota(jnp.int32, (lb, lb), 1)
    shift = ck.bit_length() - 1
    causal = jnp.logical_and(ci <= ri, jnp.right_shift(ri, shift) == jnp.right_shift(ci, shift))
    tril = jnp.where(causal, 1.0, 0.0).astype(BF)
    nt = (((1,), (1,)), ((), ()))
    tn = (((0,), (0,)), ((), ()))
    qscale = dk ** -0.5

    la = la_ref[...]
    hi = la.astype(BF)
    r1 = la - hi.astype(F32)
    mid = r1.astype(BF)
    lo = (r1 - mid.astype(F32)).astype(BF)
    b = (jnp.dot(tril, hi, preferred_element_type=F32)
         + jnp.dot(tril, mid, preferred_element_type=F32)
         + jnp.dot(tril, lo, preferred_element_type=F32))
    chunks = [slice(c * ck, (c + 1) * ck) for c in range(nchunk)]
    b_last = [b[(c + 1) * ck - 1:(c + 1) * ck, :] for c in range(nchunk)]
    eb = jnp.exp(b)
    enb = jnp.exp(-b)
    ee = jnp.concatenate([jnp.exp(b_last[c] - b[chunks[c], :]) for c in range(nchunk)], axis=0)
    dec = [jnp.exp(b_last[c]) for c in range(nchunk)]
    for h in range(GLA_HEADS):
        sk = slice(h * dk, (h + 1) * dk)
        sv = slice(h * dv, (h + 1) * dv)
        q = q_ref[:, sk].astype(F32) * qscale
        k = k_ref[:, sk].astype(F32)
        v = v_ref[:, sv]
        qt = (q * eb[:, sk]).astype(BF)
        kt = (k * enb[:, sk]).astype(BF)
        ke = (k * ee[:, sk]).astype(BF)
        a = lax.dot_general(qt, kt, nt, preferred_element_type=F32)
        a = jnp.where(causal, a, 0.0).astype(BF)
        o = jnp.dot(a, v, preferred_element_type=F32)
        st = st_ref[h]
        inter = []
        for c in range(nchunk):
            inter.append(lax.dot_general(qt[chunks[c], :], st.astype(BF), nt, preferred_element_type=F32))
            st = st * dec[c][:, sk] + lax.dot_general(v[chunks[c], :], ke[chunks[c], :], tn,
                                                      preferred_element_type=F32)
        st_ref[h] = st
        o = o + jnp.concatenate(inter, axis=0)
        on = o * lax.rsqrt(jnp.mean(o * o, axis=-1, keepdims=True) + EPS)
        r = r_ref[:, sv].astype(F32)
        o_ref[:, sv] = (on * nw_ref[:, sv] * (r * jax.nn.sigmoid(r))).astype(BF)


def _mix_body(sink_ref, x_ref, ga_ref, gb_ref, gc_ref, ba_ref, bb_ref, bc_ref, uc_ref,
              q_ref, kp_ref, kc_ref, vp_ref, vc_ref, gq_ref, gk_ref, gv_ref, gr_ref, la_ref, nw_ref,
              wpw_ref, wao_ref, wgo_ref, wout_ref, o_ref, ua_ref, ug_ref, st_ref, *, tiles_per_seq):
    first = pl.program_id(0) % tiles_per_seq == 0

    @pl.when(first)
    def _():
        st_ref[...] = jnp.zeros_like(st_ref)

    tm = x_ref.shape[0]
    _attn_tile(first, sink_ref, q_ref, kp_ref, kc_ref, vp_ref, vc_ref, ua_ref, tm // ATT_BLOCK)
    _gla_block(gq_ref, gk_ref, gv_ref, gr_ref, la_ref, nw_ref, ug_ref, st_ref, tm // GLA_CHUNK)

    def gate(p_ref, b_ref):
        return jax.nn.sigmoid(p_ref[...].astype(F32) + b_ref[...])

    m = gate(ga_ref, ba_ref) * jnp.dot(uc_ref[...], wpw_ref[...], preferred_element_type=F32)
    m = m + gate(gb_ref, bb_ref) * jnp.dot(ua_ref[...], wao_ref[...], preferred_element_type=F32)
    m = m + gate(gc_ref, bc_ref) * jnp.dot(ug_ref[...], wgo_ref[...], preferred_element_type=F32)
    o_ref[...] = x_ref[...] + jnp.dot(m.astype(BF), wout_ref[...], preferred_element_type=F32)


def _mix(x, proj, la, B, S, b_gate, uc, sinks, gla_norm_w, w_pw, w_ao, w_go, w_out):
    T, D = x.shape
    tm = MIX_TM
    tiles_per_seq = S // tm
    per_tile = tm // ATT_BLOCK
    const = lambda shape: pl.BlockSpec(shape, lambda i: (0, 0), pipeline_mode=pl.Buffered(1))
    act = lambda w, off: pl.BlockSpec((tm, w), lambda i: (i, off // w))
    before = lambda off: pl.BlockSpec(
        (ATT_BLOCK, ATT_KV_DIM), lambda i: (jnp.maximum(i * per_tile - 1, 0), off // ATT_KV_DIM))
    bias = lambda c: pl.BlockSpec((1, D), lambda i: (0, c))
    bg = b_gate.reshape(1, 3 * D)
    return pl.pallas_call(
        functools.partial(_mix_body, tiles_per_seq=tiles_per_seq),
        grid=(T // tm,),
        in_specs=[
            pl.BlockSpec(memory_space=pltpu.SMEM),
            act(D, 0),
            act(D, COL_GATE), act(D, COL_GATE + D), act(D, COL_GATE + 2 * D),
            bias(0), bias(1), bias(2),
            act(CONV_CH, 0),
            act(ATT_Q_DIM, COL_Q),
            before(COL_K), act(ATT_KV_DIM, COL_K), before(COL_V), act(ATT_KV_DIM, COL_V),
            act(GLA_K_DIM, COL_GQ), act(GLA_K_DIM, COL_GK), act(GLA_V_DIM, COL_GV), act(GLA_V_DIM, COL_GR),
            act(GLA_K_DIM, 0),
            pl.BlockSpec((1, GLA_V_DIM), lambda i: (0, 0)),
            const((CONV_CH, D)), const((ATT_Q_DIM, D)), const((GLA_V_DIM, D)), const((D, D)),
        ],
        out_specs=act(D, 0),
        out_shape=jax.ShapeDtypeStruct((T, D), F32),
        scratch_shapes=[pltpu.VMEM((tm, ATT_Q_DIM), BF), pltpu.VMEM((tm, GLA_V_DIM), BF),
                        pltpu.VMEM((GLA_HEADS, GLA_DV, GLA_DK), F32)],
        compiler_params=_params("arbitrary"),
        name="mix",
    )(sinks, x, proj, proj, proj, bg, bg, bg, uc, proj, proj, proj, proj, proj,
      proj, proj, proj, proj, la, gla_norm_w.reshape(1, GLA_V_DIM), w_pw, w_ao, w_go, w_out)


def _pair_heads(w, axis):
    group = ATT_HEADS // ATT_KV_HEADS
    split = (ATT_KV_HEADS // 2, 2, group, ATT_HEAD_DIM)
    if axis == 0:
        return w.reshape(*split, w.shape[1]).transpose(0, 2, 1, 3, 4).reshape(w.shape)
    return w.reshape(w.shape[0], *split).transpose(0, 1, 3, 2, 4).reshape(w.shape)


def _regroup_body(w_ref, main_ref, wa_ref):
    c = 2 * CONV_CH
    o_q = c
    o_k = o_q + ATT_Q_DIM
    o_v = o_k + ATT_KV_DIM
    o_gq = o_v + ATT_KV_DIM
    o_gk = o_gq + GLA_K_DIM
    o_gv = o_gk + GLA_K_DIM
    o_gr = o_gv + GLA_V_DIM
    o_ga = o_gr + GLA_V_DIM
    o_gate = o_ga + GLA_RANK

    def put(dst, src, n):
        main_ref[dst:dst + n, :] = w_ref[src:src + n, :].astype(BF)

    put(COL_GATE, o_gate, 3 * D_MODEL)
    put(COL_CONV, 0, c)
    group, dh = ATT_HEADS // ATT_KV_HEADS, ATT_HEAD_DIM
    for col in range(ATT_HEADS // 2):
        p, g = divmod(col, group)
        for side in range(2):
            h = (2 * p + side) * group + g
            put(COL_Q + (2 * col + side) * dh, o_q + h * dh, dh)
    put(COL_GV, o_gv, GLA_V_DIM)
    put(COL_GR, o_gr, GLA_V_DIM)
    put(COL_GQ, o_gq, GLA_K_DIM)
    put(COL_GK, o_gk, GLA_K_DIM)
    put(COL_K, o_k, ATT_KV_DIM)
    put(COL_V, o_v, ATT_KV_DIM)
    cols = w_ref.shape[1]
    wa_ref[...] = jnp.concatenate(
        [w_ref[o_ga:o_gate, :], jnp.zeros((LANES - GLA_RANK, cols), F32)], axis=0).astype(BF)


def _regroup_w_in(w_in, layer):
    w_t = jnp.swapaxes(w_in, 1, 2)
    _, d_in, D = w_t.shape
    tc = REGROUP_COLS
    return pl.pallas_call(
        _regroup_body,
        grid=(D // tc,),
        in_specs=[pl.BlockSpec((None, d_in, tc), lambda i: (layer, 0, i))],
        out_specs=[pl.BlockSpec((N_PROJ, tc), lambda i: (0, i)),
                   pl.BlockSpec((LANES, tc), lambda i: (0, i))],
        out_shape=[jax.ShapeDtypeStruct((N_PROJ, D), BF), jax.ShapeDtypeStruct((LANES, D), BF)],
        compiler_params=_params("parallel"),
        name="regroup",
    )(w_t)


def kernel(x, ffn1_norm, ffn1_w_gate, ffn1_w_up, ffn1_w_down, mix_norm, w_in, b_gate, conv_dw_w, conv_dw_b, conv_ln_w, conv_ln_b, conv_w_pw, att_sinks, att_w_o, gla_w_alpha, gla_b_alpha, gla_norm_w, gla_w_o, w_out, ffn2_norm, ffn2_w_gate, ffn2_w_up, ffn2_w_down, final_norm):
    B, S, D = x.shape
    xt = x.reshape(B * S, D)
    for l in range(DEPTH):
        xt = _ffn(xt, ffn1_norm[l], ffn1_w_gate, ffn1_w_up, ffn1_w_down, l)
        w_main, w_a = _regroup_w_in(w_in, l)
        w_alpha = jnp.pad(gla_w_alpha[l], ((0, LANES - GLA_RANK), (0, 0))).astype(BF)
        proj, la = _inproj(xt, mix_norm[l], w_main, w_a, w_alpha, gla_b_alpha[l])
        uc = _conv(proj, B, S, conv_dw_w[l], conv_dw_b[l], conv_ln_w[l], conv_ln_b[l])
        xt = _mix(xt, proj, la, B, S, b_gate[l], uc, att_sinks[l], gla_norm_w[l], conv_w_pw[l].astype(BF),
                    _pair_heads(att_w_o[l], axis=0).astype(BF),
                    gla_w_o[l].astype(BF), w_out[l].astype(BF))
        xt = _ffn(xt, ffn2_norm[l], ffn2_w_gate, ffn2_w_up, ffn2_w_down, l,
                  final_w=final_norm if l == DEPTH - 1 else None)
    return xt.reshape(B, S, D)
```

```python
import functools

import jax
import jax.numpy as jnp
from jax import lax
from jax.experimental import pallas as pl
from jax.experimental.pallas import tpu as pltpu

BF = jnp.bfloat16
F32 = jnp.float32

D_MODEL = 2048
DEPTH = 2
CONV_CH = 1024
CONV_WIDTH = 31
ATT_HEADS = 16
ATT_KV_HEADS = 4
ATT_HEAD_DIM = 64
ATT_BLOCK = 128
ATT_Q_DIM = 1024
ATT_KV_DIM = 256
GLA_HEADS = 4
GLA_DK = 128
GLA_DV = 256
GLA_K_DIM = 512
GLA_V_DIM = 1024
GLA_RANK = 16
GLA_GATE_NORM = 16.0
GLA_CHUNK = 64
D_FF = 5632
EPS = 1e-6

VMEM_LIMIT_BYTES = 60 * 1024 * 1024
LANES = 128
SUBLANES = 8

N_PROJ = 12800
COL_GATE = 0
COL_CONV = 6144
COL_Q = 8192
COL_GV = 9216
COL_GR = 10240
COL_GQ = 11264
COL_GK = 11776
COL_K = 12288
COL_V = 12544

FFN_TM = 1024
FFN_TF = 512
FFN_NORM_ROWS = 256
PROJ_TM = 1024
PROJ_TN = 1280
CONV_TS = 512
CONV_HALO = 32
CONV_ROWS = 32
MIX_TM = 256
REGROUP_COLS = 256


def _params(*sem):
    return pltpu.CompilerParams(dimension_semantics=sem, vmem_limit_bytes=VMEM_LIMIT_BYTES)


def _rms(x):
    return x * lax.rsqrt(jnp.mean(x * x, axis=-1, keepdims=True) + EPS)


def _ffn_body(x_hbm, nw_ref, wg_hbm, wu_hbm, wd_hbm, *rest, final, layer, tm, tf, n_i, n_j):
    if final:
        fw_ref, o_hbm, acc_ref, h_ref, wg_buf, wu_buf, wd_buf, wsem, xsem, osem = rest
    else:
        o_hbm, acc_ref, h_ref, wg_buf, wu_buf, wd_buf, wsem, xsem, osem = rest
    i = pl.program_id(0)
    j = pl.program_id(1)
    step = i * n_j + j
    buf = i % 2
    slot = step % 2

    def weight_copies(jj, sl):
        cols = pl.ds(pl.multiple_of(jj * tf, tf), tf)
        return (pltpu.make_async_copy(wg_hbm.at[layer, :, cols], wg_buf.at[sl], wsem.at[sl, 0]),
                pltpu.make_async_copy(wu_hbm.at[layer, :, cols], wu_buf.at[sl], wsem.at[sl, 1]),
                pltpu.make_async_copy(wd_hbm.at[layer, cols, :], wd_buf.at[sl], wsem.at[sl, 2]))

    def tile_rows(ii):
        return pl.ds(pl.multiple_of(ii * tm, tm), tm)

    def x_copy(ii, b):
        return pltpu.make_async_copy(x_hbm.at[tile_rows(ii), :], acc_ref.at[b], xsem.at[b])

    def out_copy(ii, b):
        return pltpu.make_async_copy(acc_ref.at[b], o_hbm.at[tile_rows(ii), :], osem.at[b])

    @pl.when(step == 0)
    def _():
        x_copy(0, 0).start()
        for cp in weight_copies(0, 0):
            cp.start()

    @pl.when(step + 1 < n_i * n_j)
    def _():
        j_next = jnp.where(j + 1 == n_j, 0, j + 1)
        for k, cp in enumerate(weight_copies(j_next, 1 - slot)):
            cp.start(priority=k % 2)

    @pl.when(j == 1)
    def _():
        @pl.when(i >= 1)
        def _():
            out_copy(i - 1, 1 - buf).wait()

        @pl.when(i + 1 < n_i)
        def _():
            x_copy(i + 1, 1 - buf).start()

    for cp in weight_copies(j, slot):
        cp.wait()

    def swiglu_into_acc(rows, h, wg, wu, wd):
        g = jnp.dot(h, wg, preferred_element_type=F32)
        u = jnp.dot(h, wu, preferred_element_type=F32)
        a = (0.5 * (g * jax.nn.sigmoid(g) * u)).astype(BF)
        acc_ref[buf, rows, :] += jnp.dot(a, wd, preferred_element_type=F32)

    def bf16_weights():
        return wg_buf[slot].astype(BF), wu_buf[slot].astype(BF), wd_buf[slot].astype(BF)

    @pl.when(j == 0)
    def _():
        x_copy(i, buf).wait()
        weights = bf16_weights()
        chunk = min(FFN_NORM_ROWS, tm)
        for r in range(tm // chunk):
            rows = pl.ds(r * chunk, chunk)
            h = (_rms(acc_ref[buf, rows, :]) * nw_ref[...]).astype(BF)
            h_ref[rows, :] = h
            swiglu_into_acc(rows, h, *weights)

    @pl.when(j > 0)
    def _():
        swiglu_into_acc(slice(None), h_ref[...], *bf16_weights())

    @pl.when(j == n_j - 1)
    def _():
        if final:
            acc_ref[buf] = _rms(acc_ref[buf]) * fw_ref[...]
        out_copy(i, buf).start()

    @pl.when(step == n_i * n_j - 1)
    def _():
        out_copy(i, buf).wait()


def _ffn(x, nw, wg, wu, wd, layer, final_w=None):
    T, D = x.shape
    F = wg.shape[2]
    tm, tf = min(FFN_TM, T), FFN_TF
    n_i, n_j = T // tm, F // tf
    assert n_j >= 2, "the buffer hand-over happens at d_ff step 1"
    final = final_w is not None
    hbm = pl.BlockSpec(memory_space=pl.ANY)
    vec = pl.BlockSpec((1, D), lambda i, j: (0, 0))
    in_specs = [hbm, vec, hbm, hbm, hbm]
    args = [x, nw.reshape(1, D), wg, wu, wd]
    if final:
        in_specs.append(vec)
        args.append(final_w.reshape(1, D))
    return pl.pallas_call(
        functools.partial(_ffn_body, final=final, layer=layer, tm=tm, tf=tf, n_i=n_i, n_j=n_j),
        grid=(n_i, n_j),
        in_specs=in_specs,
        out_specs=hbm,
        out_shape=jax.ShapeDtypeStruct((T, D), F32),
        scratch_shapes=[
            pltpu.VMEM((2, tm, D), F32),
            pltpu.VMEM((tm, D), BF),
            pltpu.VMEM((2, D, tf), F32),
            pltpu.VMEM((2, D, tf), F32),
            pltpu.VMEM((2, tf, D), F32),
            pltpu.SemaphoreType.DMA((2, 3)),
            pltpu.SemaphoreType.DMA((2,)),
            pltpu.SemaphoreType.DMA((2,)),
        ],
        compiler_params=_params("arbitrary", "arbitrary"),
        name="ffn",
    )(*args)


def _inproj_body(x_ref, nw_ref, w_hbm, wa_ref, wal_ref, bal_ref, p_ref, la_ref, h_ref, w_buf, wsem,
                 *, tn, n_i, n_j):
    i = pl.program_id(0)
    j = pl.program_id(1)
    step = i * n_j + j
    nt = (((1,), (1,)), ((), ()))

    def w_copy(jj, slot):
        rows = pl.ds(pl.multiple_of(jj * tn, tn), tn)
        return pltpu.make_async_copy(w_hbm.at[rows, :], w_buf.at[slot], wsem.at[slot])

    @pl.when(step == 0)
    def _():
        w_copy(0, 0).start()
        w_copy(1, 1).start()

    @pl.when(step + 2 < n_i * n_j)
    def _():
        j_ahead = jnp.where(j + 2 >= n_j, j + 2 - n_j, j + 2)
        w_copy(j_ahead, (step + 2) % 3).start()

    @pl.when(j == 0)
    def _():
        h = (_rms(x_ref[...]) * nw_ref[...]).astype(BF)
        h_ref[...] = h
        a_low = lax.dot_general(h, wa_ref[...], nt, preferred_element_type=F32)
        z = jnp.dot(a_low.astype(BF), wal_ref[...], preferred_element_type=F32) + bal_ref[...]
        la_ref[...] = (jnp.minimum(z, 0.0) - jnp.log1p(jnp.exp(-jnp.abs(z)))) / GLA_GATE_NORM

    slot = step % 3
    w_copy(j, slot).wait()
    p_ref[...] = lax.dot_general(h_ref[...], w_buf[slot], nt, preferred_element_type=F32).astype(BF)


def _inproj(x, nw, w, wa, wal, bal):
    T, D = x.shape
    tm, tn = min(PROJ_TM, T), PROJ_TN
    n_i, n_j = T // tm, N_PROJ // tn
    assert n_j >= 2, "the ring is primed with the first two weight tiles"
    return pl.pallas_call(
        functools.partial(_inproj_body, tn=tn, n_i=n_i, n_j=n_j),
        grid=(n_i, n_j),
        in_specs=[
            pl.BlockSpec((tm, D), lambda i, j: (i, 0)),
            pl.BlockSpec((1, D), lambda i, j: (0, 0)),
            pl.BlockSpec(memory_space=pl.ANY),
            pl.BlockSpec((LANES, D), lambda i, j: (0, 0)),
            pl.BlockSpec((LANES, GLA_K_DIM), lambda i, j: (0, 0)),
            pl.BlockSpec((1, GLA_K_DIM), lambda i, j: (0, 0)),
        ],
        out_specs=[
            pl.BlockSpec((tm, tn), lambda i, j: (i, j)),
            pl.BlockSpec((tm, GLA_K_DIM), lambda i, j: (i, 0)),
        ],
        out_shape=[
            jax.ShapeDtypeStruct((T, N_PROJ), BF),
            jax.ShapeDtypeStruct((T, GLA_K_DIM), F32),
        ],
        scratch_shapes=[pltpu.VMEM((tm, D), BF), pltpu.VMEM((3, tn, D), BF),
                        pltpu.SemaphoreType.DMA((3,))],
        compiler_params=_params("arbitrary", "arbitrary"),
        name="inproj",
    )(x, nw.reshape(1, D), w, wa, wal, bal.reshape(1, GLA_K_DIM))


def _conv_body(a_ref, g_ref, w_ref, b_ref, lnw_ref, lnb_ref, o_ref, zs_ref, wb_ref, y_ref, *, ts):
    s = pl.program_id(1)
    halo, rows = CONV_HALO, CONV_ROWS

    @pl.when(s == 0)
    def _():
        zs_ref[:, 0:halo, :] = jnp.zeros((SUBLANES, halo, CONV_CH), F32)
        for j in range(CONV_WIDTH):
            wb_ref[j] = jnp.broadcast_to(w_ref[j:j + 1, :], (SUBLANES, CONV_CH))

    @pl.when(s > 0)
    def _():
        zs_ref[:, 0:halo, :] = zs_ref[:, ts:ts + halo, :]

    a = a_ref[...].astype(F32)
    g = g_ref[...].astype(F32)
    zs_ref[0, halo:halo + ts, :] = a * jax.nn.sigmoid(g)

    def chunk(c, carry):
        r0 = pl.multiple_of(c * rows, rows)
        win = zs_ref[0, pl.ds(halo - SUBLANES + r0, rows + SUBLANES), :]
        for p in range(1, SUBLANES):
            zs_ref[p, pl.ds(halo + r0, rows), :] = win[SUBLANES - p:SUBLANES - p + rows, :]
        accs = [jnp.zeros((SUBLANES, CONV_CH), F32) + b_ref[...] for _ in range(rows // SUBLANES)]
        for d in range(CONV_WIDTH):
            wj = wb_ref[CONV_WIDTH - 1 - d]
            base = halo + r0 - SUBLANES * (d // SUBLANES)
            for i in range(rows // SUBLANES):
                zd = zs_ref[d % SUBLANES, pl.ds(base + i * SUBLANES, SUBLANES), :]
                accs[i] = accs[i] + wj * zd
        for i in range(rows // SUBLANES):
            y_ref[pl.ds(r0 + i * SUBLANES, SUBLANES), :] = accs[i]
        return carry

    lax.fori_loop(0, ts // rows, chunk, 0)

    y = y_ref[...]
    mu = jnp.mean(y, axis=-1, keepdims=True)
    dlt = y - mu
    var = jnp.mean(dlt * dlt, axis=-1, keepdims=True)
    zn = dlt * lax.rsqrt(var + EPS) * lnw_ref[...] + lnb_ref[...]
    o_ref[...] = (zn * jax.nn.sigmoid(zn)).astype(BF)


def _conv(proj, B, S, dw_w, dw_b, ln_w, ln_b):
    T = B * S
    ts = min(CONV_TS, S)
    nS = S // ts
    ca, cg = COL_CONV // CONV_CH, COL_CONV // CONV_CH + 1
    vec = pl.BlockSpec((1, CONV_CH), lambda b, s: (0, 0))
    return pl.pallas_call(
        functools.partial(_conv_body, ts=ts),
        grid=(B, nS),
        in_specs=[
            pl.BlockSpec((ts, CONV_CH), lambda b, s: (b * nS + s, ca)),
            pl.BlockSpec((ts, CONV_CH), lambda b, s: (b * nS + s, cg)),
            pl.BlockSpec((CONV_WIDTH, CONV_CH), lambda b, s: (0, 0)),
            vec, vec, vec,
        ],
        out_specs=pl.BlockSpec((ts, CONV_CH), lambda b, s: (b * nS + s, 0)),
        out_shape=jax.ShapeDtypeStruct((T, CONV_CH), BF),
        scratch_shapes=[pltpu.VMEM((SUBLANES, ts + CONV_HALO, CONV_CH), F32),
                        pltpu.VMEM((CONV_WIDTH, SUBLANES, CONV_CH), F32),
                        pltpu.VMEM((ts, CONV_CH), F32)],
        compiler_params=_params("parallel", "arbitrary"),
        name="conv",
    )(proj, proj, dw_w.reshape(CONV_WIDTH, CONV_CH), dw_b.reshape(1, CONV_CH),
      ln_w.reshape(1, CONV_CH), ln_b.reshape(1, CONV_CH))


def _attn_tile(first, sink_ref, q_ref, kp_ref, kc_ref, vp_ref, vc_ref, o_ref, nblk):
    blk, dh = ATT_BLOCK, ATT_HEAD_DIM
    group = ATT_HEADS // ATT_KV_HEADS
    qi = lax.broadcasted_iota(jnp.int32, (blk, blk), 0)
    kj = lax.broadcasted_iota(jnp.int32, (blk, blk), 1)
    cur = kj <= qi
    dist_in = ((qi - kj) & (blk - 1)).astype(F32)
    dist_first = jnp.where(jnp.logical_and(first, kj > qi), jnp.inf, dist_in)
    cur_b = jnp.where(cur, 1.0, 0.0).astype(BF)
    prev_b = jnp.where(cur, 0.0, 1.0).astype(BF)
    lane = lax.broadcasted_iota(jnp.int32, (2 * blk, 2 * dh), 1)
    half_b = (jnp.where(lane < dh, 1.0, 0.0).astype(BF), jnp.where(lane < dh, 0.0, 1.0).astype(BF))
    qscale = jnp.asarray(dh ** -0.5, BF)
    nt = (((1,), (1,)), ((), ()))
    for n in range(nblk):
        rows = slice(n * blk, (n + 1) * blk)
        before = slice((n - 1) * blk, n * blk)
        dist = dist_first if n == 0 else dist_in
        for p in range(ATT_KV_HEADS // 2):
            col = slice(p * 2 * dh, (p + 1) * 2 * dh)
            k_prev = kp_ref[:, col] if n == 0 else kc_ref[before, col]
            v_prev = vp_ref[:, col] if n == 0 else vc_ref[before, col]
            kcat = jnp.concatenate([k_prev, kc_ref[rows, col]], axis=0)
            vcat = jnp.concatenate([v_prev, vc_ref[rows, col]], axis=0)
            k_side = [kcat * hb for hb in half_b]
            v_side = [vcat * hb for hb in half_b]
            for g in range(group):
                c = p * group + g
                qcol = slice(c * 2 * dh, (c + 1) * 2 * dh)
                q2 = q_ref[rows, qcol] * qscale
                acc = None
                for side in range(2):
                    h = (2 * p + side) * group + g
                    slope = 2.0 ** (-8.0 * (h + 1) / ATT_HEADS)
                    s = lax.dot_general(q2, k_side[side], nt, preferred_element_type=F32)
                    t = jnp.where(cur, s[:, blk:], s[:, :blk]) - slope * dist
                    sink = sink_ref[h]
                    m = jnp.maximum(jnp.max(t, axis=-1, keepdims=True), sink)
                    pe = jnp.exp(t - m)
                    den = jnp.sum(pe, axis=-1, keepdims=True) + jnp.exp(sink - m)
                    pb = pe.astype(BF)
                    pcat = jnp.concatenate([pb * prev_b, pb * cur_b], axis=1)
                    o = jnp.dot(pcat, v_side[side], preferred_element_type=F32) / den
                    acc = o if acc is None else acc + o
                o_ref[rows, qcol] = acc.astype(BF)


def _gla_block(q_ref, k_ref, v_ref, r_ref, la_ref, nw_ref, o_ref, st_ref, nchunk):
    ck, dk, dv = GLA_CHUNK, GLA_DK, GLA_DV
    lb = nchunk * ck
    ri = lax.broadcasted_iota(jnp.int32, (lb, lb), 0)
    ci = lax.broadcasted_iota(jnp.int32, (lb, lb), 1)
    shift = ck.bit_length() - 1
    causal = jnp.logical_and(ci <= ri, jnp.right_shift(ri, shift) == jnp.right_shift(ci, shift))
    tril = jnp.where(causal, 1.0, 0.0).astype(BF)
    nt = (((1,), (1,)), ((), ()))
    tn = (((0,), (0,)), ((), ()))
    qscale = dk ** -0.5

    la = la_ref[...]
    hi = la.astype(BF)
    r1 = la - hi.astype(F32)
    mid = r1.astype(BF)
    lo = (r1 - mid.astype(F32)).astype(BF)
    b = (jnp.dot(tril, hi, preferred_element_type=F32)
         + jnp.dot(tril, mid, preferred_element_type=F32)
         + jnp.dot(tril, lo, preferred_element_type=F32))
    chunks = [slice(c * ck, (c + 1) * ck) for c in range(nchunk)]
    b_last = [b[(c + 1) * ck - 1:(c + 1) * ck, :] for c in range(nchunk)]
    eb = jnp.exp(b)
    enb = jnp.exp(-b)
    ee = jnp.concatenate([jnp.exp(b_last[c] - b[chunks[c], :]) for c in range(nchunk)], axis=0)
    dec = [jnp.exp(b_last[c]) for c in range(nchunk)]
    for h in range(GLA_HEADS):
        sk = slice(h * dk, (h + 1) * dk)
        sv = slice(h * dv, (h + 1) * dv)
        q = q_ref[:, sk].astype(F32) * qscale
        k = k_ref[:, sk].astype(F32)
        v = v_ref[:, sv]
        qt = (q * eb[:, sk]).astype(BF)
        kt = (k * enb[:, sk]).astype(BF)
        ke = (k * ee[:, sk]).astype(BF)
        a = lax.dot_general(qt, kt, nt, preferred_element_type=F32)
        a = jnp.where(causal, a, 0.0).astype(BF)
        o = jnp.dot(a, v, preferred_element_type=F32)
        st = st_ref[h]
        inter = []
        for c in range(nchunk):
            inter.append(lax.dot_general(qt[chunks[c], :], st.astype(BF), nt, preferred_element_type=F32))
            st = st * dec[c][:, sk] + lax.dot_general(v[chunks[c], :], ke[chunks[c], :], tn,
                                                      preferred_element_type=F32)
        st_ref[h] = st
        o = o + jnp.concatenate(inter, axis=0)
        on = o * lax.rsqrt(jnp.mean(o * o, axis=-1, keepdims=True) + EPS)
        r = r_ref[:, sv].astype(F32)
        o_ref[:, sv] = (on * nw_ref[:, sv] * (r * jax.nn.sigmoid(r))).astype(BF)


def _mix_body(sink_ref, x_ref, ga_ref, gb_ref, gc_ref, ba_ref, bb_ref, bc_ref, uc_ref,
              q_ref, kp_ref, kc_ref, vp_ref, vc_ref, gq_ref, gk_ref, gv_ref, gr_ref, la_ref, nw_ref,
              wpw_ref, wao_ref, wgo_ref, wout_ref, o_ref, ua_ref, ug_ref, st_ref, *, tiles_per_seq):
    first = pl.program_id(0) % tiles_per_seq == 0

    @pl.when(first)
    def _():
        st_ref[...] = jnp.zeros_like(st_ref)

    tm = x_ref.shape[0]
    _attn_tile(first, sink_ref, q_ref, kp_ref, kc_ref, vp_ref, vc_ref, ua_ref, tm // ATT_BLOCK)
    _gla_block(gq_ref, gk_ref, gv_ref, gr_ref, la_ref, nw_ref, ug_ref, st_ref, tm // GLA_CHUNK)

    def gate(p_ref, b_ref):
        return jax.nn.sigmoid(p_ref[...].astype(F32) + b_ref[...])

    m = gate(ga_ref, ba_ref) * jnp.dot(uc_ref[...], wpw_ref[...], preferred_element_type=F32)
    m = m + gate(gb_ref, bb_ref) * jnp.dot(ua_ref[...], wao_ref[...], preferred_element_type=F32)
    m = m + gate(gc_ref, bc_ref) * jnp.dot(ug_ref[...], wgo_ref[...], preferred_element_type=F32)
    o_ref[...] = x_ref[...] + jnp.dot(m.astype(BF), wout_ref[...], preferred_element_type=F32)


def _mix(x, proj, la, B, S, b_gate, uc, sinks, gla_norm_w, w_pw, w_ao, w_go, w_out):
    T, D = x.shape
    tm = MIX_TM
    tiles_per_seq = S // tm
    per_tile = tm // ATT_BLOCK
    const = lambda shape: pl.BlockSpec(shape, lambda i: (0, 0), pipeline_mode=pl.Buffered(1))
    act = lambda w, off: pl.BlockSpec((tm, w), lambda i: (i, off // w))
    before = lambda off: pl.BlockSpec(
        (ATT_BLOCK, ATT_KV_DIM), lambda i: (jnp.maximum(i * per_tile - 1, 0), off // ATT_KV_DIM))
    bias = lambda c: pl.BlockSpec((1, D), lambda i: (0, c))
    bg = b_gate.reshape(1, 3 * D)
    return pl.pallas_call(
        functools.partial(_mix_body, tiles_per_seq=tiles_per_seq),
        grid=(T // tm,),
        in_specs=[
            pl.BlockSpec(memory_space=pltpu.SMEM),
            act(D, 0),
            act(D, COL_GATE), act(D, COL_GATE + D), act(D, COL_GATE + 2 * D),
            bias(0), bias(1), bias(2),
            act(CONV_CH, 0),
            act(ATT_Q_DIM, COL_Q),
            before(COL_K), act(ATT_KV_DIM, COL_K), before(COL_V), act(ATT_KV_DIM, COL_V),
            act(GLA_K_DIM, COL_GQ), act(GLA_K_DIM, COL_GK), act(GLA_V_DIM, COL_GV), act(GLA_V_DIM, COL_GR),
            act(GLA_K_DIM, 0),
            pl.BlockSpec((1, GLA_V_DIM), lambda i: (0, 0)),
            const((CONV_CH, D)), const((ATT_Q_DIM, D)), const((GLA_V_DIM, D)), const((D, D)),
        ],
        out_specs=act(D, 0),
        out_shape=jax.ShapeDtypeStruct((T, D), F32),
        scratch_shapes=[pltpu.VMEM((tm, ATT_Q_DIM), BF), pltpu.VMEM((tm, GLA_V_DIM), BF),
                        pltpu.VMEM((GLA_HEADS, GLA_DV, GLA_DK), F32)],
        compiler_params=_params("arbitrary"),
        name="mix",
    )(sinks, x, proj, proj, proj, bg, bg, bg, uc, proj, proj, proj, proj, proj,
      proj, proj, proj, proj, la, gla_norm_w.reshape(1, GLA_V_DIM), w_pw, w_ao, w_go, w_out)


def _pair_heads(w, axis):
    group = ATT_HEADS // ATT_KV_HEADS
    split = (ATT_KV_HEADS // 2, 2, group, ATT_HEAD_DIM)
    if axis == 0:
        return w.reshape(*split, w.shape[1]).transpose(0, 2, 1, 3, 4).reshape(w.shape)
    return w.reshape(w.shape[0], *split).transpose(0, 1, 3, 2, 4).reshape(w.shape)


def _regroup_body(w_ref, main_ref, wa_ref):
    c = 2 * CONV_CH
    o_q = c
    o_k = o_q + ATT_Q_DIM
    o_v = o_k + ATT_KV_DIM
    o_gq = o_v + ATT_KV_DIM
    o_gk = o_gq + GLA_K_DIM
    o_gv = o_gk + GLA_K_DIM
    o_gr = o_gv + GLA_V_DIM
    o_ga = o_gr + GLA_V_DIM
    o_gate = o_ga + GLA_RANK

    def put(dst, src, n):
        main_ref[dst:dst + n, :] = w_ref[src:src + n, :].astype(BF)

    put(COL_GATE, o_gate, 3 * D_MODEL)
    put(COL_CONV, 0, c)
    group, dh = ATT_HEADS // ATT_KV_HEADS, ATT_HEAD_DIM
    for col in range(ATT_HEADS // 2):
        p, g = divmod(col, group)
        for side in range(2):
            h = (2 * p + side) * group + g
            put(COL_Q + (2 * col + side) * dh, o_q + h * dh, dh)
    put(COL_GV, o_gv, GLA_V_DIM)
    put(COL_GR, o_gr, GLA_V_DIM)
    put(COL_GQ, o_gq, GLA_K_DIM)
    put(COL_GK, o_gk, GLA_K_DIM)
    put(COL_K, o_k, ATT_KV_DIM)
    put(COL_V, o_v, ATT_KV_DIM)
    cols = w_ref.shape[1]
    wa_ref[...] = jnp.concatenate(
        [w_ref[o_ga:o_gate, :], jnp.zeros((LANES - GLA_RANK, cols), F32)], axis=0).astype(BF)


def _regroup_w_in(w_in, layer):
    w_t = jnp.swapaxes(w_in, 1, 2)
    _, d_in, D = w_t.shape
    tc = REGROUP_COLS
    return pl.pallas_call(
        _regroup_body,
        grid=(D // tc,),
        in_specs=[pl.BlockSpec((None, d_in, tc), lambda i: (layer, 0, i))],
        out_specs=[pl.BlockSpec((N_PROJ, tc), lambda i: (0, i)),
                   pl.BlockSpec((LANES, tc), lambda i: (0, i))],
        out_shape=[jax.ShapeDtypeStruct((N_PROJ, D), BF), jax.ShapeDtypeStruct((LANES, D), BF)],
        compiler_params=_params("parallel"),
        name="regroup",
    )(w_t)


def kernel(x, ffn1_norm, ffn1_w_gate, ffn1_w_up, ffn1_w_down, mix_norm, w_in, b_gate, conv_dw_w, conv_dw_b, conv_ln_w, conv_ln_b, conv_w_pw, att_sinks, att_w_o, gla_w_alpha, gla_b_alpha, gla_norm_w, gla_w_o, w_out, ffn2_norm, ffn2_w_gate, ffn2_w_up, ffn2_w_down, final_norm):
    B, S, D = x.shape
    xt = x.reshape(B * S, D)
    for l in range(DEPTH):
        xt = _ffn(xt, ffn1_norm[l], ffn1_w_gate, ffn1_w_up, ffn1_w_down, l)
        w_main, w_a = _regroup_w_in(w_in, l)
        w_alpha = jnp.pad(gla_w_alpha[l], ((0, LANES - GLA_RANK), (0, 0))).astype(BF)
        proj, la = _inproj(xt, mix_norm[l], w_main, w_a, w_alpha, gla_b_alpha[l])
        uc = _conv(proj, B, S, conv_dw_w[l], conv_dw_b[l], conv_ln_w[l], conv_ln_b[l])
        xt = _mix(xt, proj, la, B, S, b_gate[l], uc, att_sinks[l], gla_norm_w[l], conv_w_pw[l].astype(BF),
                    _pair_heads(att_w_o[l], axis=0).astype(BF),
                    gla_w_o[l].astype(BF), w_out[l].astype(BF))
        xt = _ffn(xt, ffn2_norm[l], ffn2_w_gate, ffn2_w_up, ffn2_w_down, l,
                  final_w=final_norm if l == DEPTH - 1 else None)
    return xt.reshape(B, S, D)
```
